```python
import jax, jax.numpy as jnp
from jax import lax
import numpy as np

D_MODEL = 1024
BATCH = 2
SEQ = 8192
DEPTH = 4
DEC_BATCH = 32
DEC_SEQ = 4
PAST_LEN = 8192
PAGE_SIZE = 128

N_EVEN = (DEPTH + 1) // 2
N_ODD = DEPTH // 2
H_RET = 4
DK_RET = 128
DV_RET = 128
W_RET = H_RET * DV_RET
RET_CHUNK = 128
ROPE_BASE = 10000.0
G_SG = 4
C_SG = 128
W_SG = G_SG * C_SG
SG_CHUNK = 128
W_IN_EVEN = 2 * H_RET * DK_RET + 2 * W_RET + 2 * W_SG
W_MIX_EVEN = W_RET + W_SG
H_SB = 8
DH_SB = 128
W_SB = H_SB * DH_SB
Q_BLOCK = 128
SB_BIAS_INIT = -6.0
N_MEM = 256
H_X = 4
DH_X = 128
W_X = H_X * DH_X
D_FF = 2816
N_EXPERTS = 8
TOP_K = 2
D_FF_E = 1408
EPS = 1e-6
F32 = jnp.float32

kernel_name = "hybrid_retention_sgmlp_stickbreak_step"


def rms_norm(x, g):
    xf = x.astype(F32)
    y = xf * lax.rsqrt(jnp.mean(xf * xf, axis=-1, keepdims=True) + EPS)
    return (y * g.astype(F32)).astype(x.dtype)


def head_rms(x):
    xf = x.astype(F32)
    return (xf * lax.rsqrt(jnp.mean(xf * xf, axis=-1, keepdims=True) + EPS)).astype(x.dtype)


def layer_norm(x, g):
    xf = x.astype(F32)
    xc = xf - jnp.mean(xf, axis=-1, keepdims=True)
    y = xc * lax.rsqrt(jnp.mean(xc * xc, axis=-1, keepdims=True) + EPS)
    return (y * g.astype(F32)).astype(x.dtype)


def rotary(x, pos):
    half = x.shape[-1] // 2
    inv = 1.0 / (ROPE_BASE ** jnp.linspace(0.0, 1.0, half, dtype=F32))
    ang = pos[:, None] * inv[None, :]
    cos = jnp.cos(ang)[None, :, None, :]
    sin = jnp.sin(ang)[None, :, None, :]
    xf = x.astype(F32)
    x1, x2 = xf[..., :half], xf[..., half:]
    return jnp.concatenate([x1 * cos - x2 * sin, x2 * cos + x1 * sin], axis=-1).astype(x.dtype)


def split_cols(h, sizes):
    return jnp.split(h, [int(c) for c in np.cumsum(sizes)[:-1]], axis=-1)


def retention(q, k, v, s0, chunk):
    B, L, H, dk = q.shape
    dv = v.shape[-1]
    n = L // chunk
    log_g = jnp.log(1.0 - 2.0 ** (-5.0 - jnp.arange(H, dtype=F32)))
    idx = jnp.arange(chunk, dtype=F32)
    diff = idx[:, None] - idx[None, :]
    dmask = jnp.where(diff >= 0, jnp.exp(log_g[:, None, None] * jnp.maximum(diff, 0.0)), 0.0)
    qc = q.astype(F32).reshape(B, n, chunk, H, dk)
    kc = k.astype(F32).reshape(B, n, chunk, H, dk)
    vc = v.astype(F32).reshape(B, n, chunk, H, dv)
    intra = jnp.einsum('bnihd,bnjhd->bnhij', qc, kc) * dmask
    o_intra = jnp.einsum('bnhij,bnjhe->bnihe', intra, vc)
    k_dec = jnp.exp(log_g[None, :] * (chunk - 1.0 - idx)[:, None])
    kv = jnp.einsum('bnjhd,jh,bnjhe->bnhde', kc, k_dec, vc)
    g_chunk = jnp.exp(log_g * chunk)[None, :, None, None]

    def step(s, kv_n):
        return g_chunk * s + kv_n, s

    s_last, s_prev = lax.scan(step, s0.astype(F32), jnp.moveaxis(kv, 1, 0))
    s_prev = jnp.moveaxis(s_prev, 0, 1)
    q_dec = jnp.exp(log_g[None, :] * (idx + 1.0)[:, None])
    o_cross = jnp.einsum('bnihd,ih,bnhde->bnihe', qc, q_dec, s_prev)
    o = (o_intra + o_cross).reshape(B, L, H, dv).astype(v.dtype)
    return o, s_last


def spatial_gate(u, v, w_s, b_s, chunk):
    B, L, G, C = u.shape
    n = L // chunk
    w = jnp.tril(w_s[:, :chunk, :chunk])
    vc = v.reshape(B, n, chunk, G, C)
    f = jnp.einsum('gij,bnjgc->bnigc', w, vc) + jnp.transpose(b_s[:, :chunk])[None, None, :, :, None]
    return (u.reshape(B, n, chunk, G, C) * f).reshape(B, L, G * C)


def even_mixer(h, w_in, w_out, sg_norm_g, sg_w_s, sg_b, s0, pos, chunk_ret, chunk_sg):
    B, L, _ = h.shape
    qa, ka, va, ga, ub, vb = split_cols(h @ w_in, [H_RET * DK_RET, H_RET * DK_RET, W_RET, W_RET, W_SG, W_SG])
    qa = rotary(qa.reshape(B, L, H_RET, DK_RET), pos)
    ka = rotary(ka.reshape(B, L, H_RET, DK_RET), pos) * (DK_RET ** -0.5)
    o, s_last = retention(qa, ka, va.reshape(B, L, H_RET, DV_RET), s0, chunk_ret)
    oa = head_rms(o).reshape(B, L, W_RET) * jax.nn.silu(ga)
    ub = jax.nn.gelu(ub, approximate=False)
    vb = layer_norm(jax.nn.gelu(vb, approximate=False), sg_norm_g)
    ob = spatial_gate(ub.reshape(B, L, G_SG, C_SG), vb.reshape(B, L, G_SG, C_SG), sg_w_s, sg_b, chunk_sg)
    y = jnp.concatenate([oa, ob], axis=-1) @ w_out
    return y, s_last, vb


def stick_breaking(q, k, v, q_pos, k_pos, bias):
    z = jnp.einsum('bthd,blhd->bhtl', q, k).astype(F32) * (q.shape[-1] ** -0.5)
    z = z + bias.astype(F32)[None, :, None, None]
    mask = k_pos[None, :] < q_pos[:, None]
    log_beta = jax.nn.log_sigmoid(z)
    log_om = jnp.where(mask, log_beta - z, 0.0)
    after = lax.cumsum(log_om, axis=3, reverse=True) - log_om
    a = jnp.where(mask, jnp.exp(log_beta + after), 0.0)
    return jnp.einsum('bhtl,blhd->bthd', a.astype(v.dtype), v)


def sb_prompt(h, w_qkv, w_o, bias):
    B, S, _ = h.shape
    q, k, v = [t.reshape(B, S, H_SB, DH_SB) for t in jnp.split(h @ w_qkv, 3, axis=-1)]
    nb = S // Q_BLOCK
    qb = jnp.moveaxis(q.reshape(B, nb, Q_BLOCK, H_SB, DH_SB), 1, 0)
    k_pos = jnp.arange(S)

    def block(args):
        qi, i = args
        return stick_breaking(qi, k, v, i * Q_BLOCK + jnp.arange(Q_BLOCK), k_pos, bias)

    o = lax.map(block, (qb, jnp.arange(nb)))
    o = jnp.moveaxis(o, 0, 1).reshape(B, S, W_SB)
    return o @ w_o, k, v


def sb_sample(h, w_qkv, w_o, bias, pool_k, pool_v, page_table):
    B, T, _ = h.shape
    q, k, v = [t.reshape(B, T, H_SB, DH_SB) for t in jnp.split(h @ w_qkv, 3, axis=-1)]
    past = page_table.shape[1] * PAGE_SIZE
    pk = pool_k[page_table].reshape(B, past, H_SB, DH_SB)
    pv = pool_v[page_table].reshape(B, past, H_SB, DH_SB)
    kk = jnp.concatenate([pk, k.astype(pk.dtype)], axis=1)
    vv = jnp.concatenate([pv, v.astype(pv.dtype)], axis=1)
    o = stick_breaking(q, kk, vv, past + jnp.arange(T), jnp.arange(past + T), bias)
    return o.reshape(B, T, W_SB) @ w_o, k, v


def mem_kv(mem, w_k, w_v, k_g):
    B, M, _ = mem.shape
    mk = rms_norm((mem @ w_k).reshape(B, M, H_X, DH_X), k_g)
    mv = (mem @ w_v).reshape(B, M, H_X, DH_X)
    return mk, mv


def cross_attn(h, mk, mv, w_q, w_o, q_g):
    B, L, _ = h.shape
    q = rms_norm((h @ w_q).reshape(B, L, H_X, DH_X), q_g)
    s = jnp.einsum('blhd,bmhd->bhlm', q, mk).astype(F32) * (DH_X ** -0.5)
    p = jax.nn.softmax(s, axis=-1)
    o = jnp.einsum('bhlm,bmhd->blhd', p.astype(mv.dtype), mv).reshape(B, L, W_X)
    return o @ w_o


def swiglu(x, w_gu, w_down):
    g, u = jnp.split(x @ w_gu, 2, axis=-1)
    return (jax.nn.silu(g) * u) @ w_down


def moe(x, w_router, w_gu, w_down):
    logits = (x @ w_router).astype(F32)
    top_v, top_i = lax.top_k(logits, TOP_K)
    w = jax.nn.softmax(top_v, axis=-1)
    gates = jnp.sum(w[..., None] * jax.nn.one_hot(top_i, N_EXPERTS, dtype=F32), axis=-2).astype(x.dtype)
    y = jnp.zeros_like(x)
    for e in range(N_EXPERTS):
        y = y + gates[..., e:e + 1] * swiglu(x, w_gu[e], w_down[e])
    return y


def setup_inputs(seed: int = 0) -> dict:
    key = jax.random.key(seed)
    ks = iter(jax.random.split(key, 40))

    def nrm(shape, scale=1.0):
        return jax.random.normal(next(ks), shape, F32) * scale

    def gain(shape):
        return 1.0 + nrm(shape, 0.02)

    n_pages = PAST_LEN // PAGE_SIZE
    n_used = DEC_BATCH * n_pages
    n_pool = (5 * n_used + 3) // 4
    page_table = jax.random.permutation(next(ks), n_pool)[:n_used].reshape(DEC_BATCH, n_pages).astype(jnp.int32)
    return {
        "x_prompt": nrm((BATCH, SEQ, D_MODEL)),
        "x_sample": nrm((DEC_BATCH, DEC_SEQ, D_MODEL)),
        "mem_prompt": nrm((BATCH, N_MEM, D_MODEL)),
        "state_ret": nrm((N_EVEN, DEC_BATCH, H_RET, DK_RET, DV_RET)),
        "cache_sb_k": nrm((N_ODD, n_pool, PAGE_SIZE, H_SB, DH_SB)),
        "cache_sb_v": nrm((N_ODD, n_pool, PAGE_SIZE, H_SB, DH_SB)),
        "page_table": page_table,
        "cache_mem_k": nrm((DEPTH, DEC_BATCH, N_MEM, H_X, DH_X)),
        "cache_mem_v": nrm((DEPTH, DEC_BATCH, N_MEM, H_X, DH_X)),
        "norm_mix_g": gain((DEPTH, D_MODEL)),
        "norm_x_g": gain((DEPTH, D_MODEL)),
        "norm_ffn_g": gain((DEPTH, D_MODEL)),
        "w_in_even": nrm((N_EVEN, D_MODEL, W_IN_EVEN), D_MODEL ** -0.5),
        "w_out_even": nrm((N_EVEN, W_MIX_EVEN, D_MODEL), W_MIX_EVEN ** -0.5),
        "sg_norm_g": gain((N_EVEN, W_SG)),
        "sg_w_s": nrm((N_EVEN, G_SG, SG_CHUNK, SG_CHUNK), SG_CHUNK ** -0.5),
        "sg_b": 1.0 + nrm((N_EVEN, G_SG, SG_CHUNK), 0.02),
        "w_qkv_sb": nrm((N_ODD, D_MODEL, 3 * W_SB), D_MODEL ** -0.5),
        "w_o_sb": nrm((N_ODD, W_SB, D_MODEL), W_SB ** -0.5),
        "sb_bias": SB_BIAS_INIT + nrm((N_ODD, H_SB), 0.1),
        "w_q_x": nrm((DEPTH, D_MODEL, W_X), D_MODEL ** -0.5),
        "w_k_x": nrm((DEPTH, D_MODEL, W_X), D_MODEL ** -0.5),
        "w_v_x": nrm((DEPTH, D_MODEL, W_X), D_MODEL ** -0.5),
        "w_o_x": nrm((DEPTH, W_X, D_MODEL), W_X ** -0.5),
        "q_norm_x_g": gain((DEPTH, DH_X)),
        "k_norm_x_g": gain((DEPTH, DH_X)),
        "w_gu_dense": nrm((N_EVEN, D_MODEL, 2 * D_FF), D_MODEL ** -0.5),
        "w_down_dense": nrm((N_EVEN, D_FF, D_MODEL), D_FF ** -0.5),
        "w_router": nrm((N_ODD, D_MODEL, N_EXPERTS), D_MODEL ** -0.5),
        "w_gu_moe": nrm((N_ODD, N_EXPERTS, D_MODEL, 2 * D_FF_E), D_MODEL ** -0.5),
        "w_down_moe": nrm((N_ODD, N_EXPERTS, D_FF_E, D_MODEL), D_FF_E ** -0.5),
    }


def reference(x_prompt, x_sample, mem_prompt, state_ret, cache_sb_k, cache_sb_v, page_table,
              cache_mem_k, cache_mem_v, norm_mix_g, norm_x_g, norm_ffn_g, w_in_even, w_out_even,
              sg_norm_g, sg_w_s, sg_b, w_qkv_sb, w_o_sb, sb_bias, w_q_x, w_k_x, w_v_x, w_o_x,
              q_norm_x_g, k_norm_x_g, w_gu_dense, w_down_dense, w_router, w_gu_moe, w_down_moe):
    pos_p = jnp.arange(SEQ, dtype=F32)
    pos_s = PAST_LEN + jnp.arange(DEC_SEQ, dtype=F32)
    xp, xs = x_prompt, x_sample
    ret_p, ret_s, sgv_s = [], [], []
    sbk_p, sbv_p, sbk_s, sbv_s = [], [], [], []
    mk_p, mv_p = [], []
    for l in range(DEPTH):
        i = l // 2
        hp = rms_norm(xp, norm_mix_g[l])
        hs = rms_norm(xs, norm_mix_g[l])
        if l % 2 == 0:
            s0 = jnp.zeros((xp.shape[0], H_RET, DK_RET, DV_RET), F32)
            yp, sp, _ = even_mixer(hp, w_in_even[i], w_out_even[i], sg_norm_g[i], sg_w_s[i], sg_b[i],
                                   s0, pos_p, RET_CHUNK, SG_CHUNK)
            ys, ss, vs = even_mixer(hs, w_in_even[i], w_out_even[i], sg_norm_g[i], sg_w_s[i], sg_b[i],
                                    state_ret[i], pos_s, DEC_SEQ, DEC_SEQ)
            ret_p.append(sp)
            ret_s.append(ss)
            sgv_s.append(vs)
        else:
            yp, kp, vp = sb_prompt(hp, w_qkv_sb[i], w_o_sb[i], sb_bias[i])
            ys, kn, vn = sb_sample(hs, w_qkv_sb[i], w_o_sb[i], sb_bias[i], cache_sb_k[i], cache_sb_v[i], page_table)
            sbk_p.append(kp)
            sbv_p.append(vp)
            sbk_s.append(kn)
            sbv_s.append(vn)
        xp = xp + yp
        xs = xs + ys
        mk, mv = mem_kv(mem_prompt, w_k_x[l], w_v_x[l], k_norm_x_g[l])
        mk_p.append(mk)
        mv_p.append(mv)
        xp = xp + cross_attn(rms_norm(xp, norm_x_g[l]), mk, mv, w_q_x[l], w_o_x[l], q_norm_x_g[l])
        xs = xs + cross_attn(rms_norm(xs, norm_x_g[l]), cache_mem_k[l], cache_mem_v[l], w_q_x[l], w_o_x[l], q_norm_x_g[l])
        hp = rms_norm(xp, norm_ffn_g[l])
        hs = rms_norm(xs, norm_ffn_g[l])
        if l % 2 == 0:
            xp = xp + swiglu(hp, w_gu_dense[i], w_down_dense[i])
            xs = xs + swiglu(hs, w_gu_dense[i], w_down_dense[i])
        else:
            xp = xp + moe(hp, w_router[i], w_gu_moe[i], w_down_moe[i])
            xs = xs + moe(hs, w_router[i], w_gu_moe[i], w_down_moe[i])
    return (xp, xs, jnp.stack(ret_p), jnp.stack(ret_s), jnp.stack(sgv_s),
            jnp.stack(sbk_p), jnp.stack(sbv_p), jnp.stack(sbk_s), jnp.stack(sbv_s),
            jnp.stack(mk_p), jnp.stack(mv_p))
```

```python
import functools
import math

import numpy as np
import jax
import jax.numpy as jnp
from jax import lax
from jax.experimental import pallas as pl
from jax.experimental.pallas import tpu as pltpu

F32 = jnp.float32
BF16 = jnp.bfloat16
EPS = 1e-6
ROPE_BASE = 10000.0
LANES = 128
VMEM_LIMIT_BYTES = 56 * 1024 * 1024
HEAD = 128
PAGE = 128
T_PAD = 8
NN = (((1,), (0,)), ((), ()))
NT = (((1,), (1,)), ((), ()))
TN = (((0,), (0,)), ((), ()))


def _cparams(*sem):
    return pltpu.CompilerParams(dimension_semantics=sem, vmem_limit_bytes=VMEM_LIMIT_BYTES)


def _rms(x):
    return x * lax.rsqrt(jnp.mean(x * x, axis=-1, keepdims=True) + EPS)


def _gelu(x):
    return 0.5 * x * (1.0 + lax.erf(x * (2.0 ** -0.5)))


def _silu(x):
    return x * jax.nn.sigmoid(x)


def _mm(a, b, dims=NN):
    assert a.dtype == b.dtype
    precision = lax.Precision.HIGHEST if a.dtype == F32 else None
    return lax.dot_general(a, b, dims, precision=precision, preferred_element_type=F32)


def _norm_mm_body(x_ref, g_ref, w_ref, o_ref, h_ref):
    @pl.when(pl.program_id(1) == 0)
    def _():
        h_ref[...] = (_rms(x_ref[...]) * g_ref[...]).astype(h_ref.dtype)

    o_ref[...] = _mm(h_ref[...], w_ref[...])


def norm_mm(x, g, w, groups, tm, tn):
    M, K = x.shape
    N = w.shape[1]
    ng = N // groups
    npg = ng // tn
    return pl.pallas_call(
        _norm_mm_body,
        grid=(M // tm, N // tn),
        in_specs=[
            pl.BlockSpec((tm, K), lambda i, j: (i, 0)),
            pl.BlockSpec((1, K), lambda i, j: (0, 0)),
            pl.BlockSpec((K, tn), lambda i, j: (0, j)),
        ],
        out_specs=pl.BlockSpec((None, tm, tn), lambda i, j: (j // npg, i, j % npg)),
        out_shape=jax.ShapeDtypeStruct((groups, M, ng), F32),
        scratch_shapes=[pltpu.VMEM((tm, K), w.dtype)],
        compiler_params=_cparams("parallel", "arbitrary"),
        name="norm_mm",
    )(x, g.reshape(1, K), w)


def _mm_res_body(a_ref, w_ref, r_ref, o_ref):
    o_ref[...] = r_ref[...] + _mm(a_ref[...], w_ref[...])


def mm_res(a, w, res, tm, tn):
    M, K = a.shape
    N = w.shape[1]
    tn = min(tn, N)
    return pl.pallas_call(
        _mm_res_body,
        grid=(M // tm, N // tn),
        in_specs=[
            pl.BlockSpec((tm, K), lambda i, j: (i, 0)),
            pl.BlockSpec((K, tn), lambda i, j: (0, j)),
            pl.BlockSpec((tm, tn), lambda i, j: (i, j)),
        ],
        out_specs=pl.BlockSpec((tm, tn), lambda i, j: (i, j)),
        out_shape=jax.ShapeDtypeStruct((M, N), F32),
        compiler_params=_cparams("parallel", "arbitrary"),
        name="mm_res",
    )(a, w, res)


def _even_core_body(*refs, n_heads, n_groups, gch, rows_in, has_s0, emit_vb):
    it = iter(refs)
    p_ref, cos_ref, sin_ref, dmask_ref, qdec_ref, kdec_ref, sgg_ref, ws_ref, sgb_ref = (next(it) for _ in range(9))
    s0_ref = next(it) if has_s0 else None
    y_ref, sout_ref = next(it), next(it)
    vb_ref = next(it) if emit_vb else None
    s_ref = next(it)
    C = HEAD
    cd = y_ref.dtype
    c = pl.program_id(1)
    w_ret = n_heads * HEAD

    @pl.when(c == 0)
    def _():
        if has_s0:
            s_ref[...] = s0_ref[...]
        else:
            s_ref[...] = jnp.zeros_like(s_ref)

    def rows(x):
        if rows_in == C:
            return x
        return jnp.concatenate([x, jnp.zeros((C - rows_in, x.shape[1]), x.dtype)], axis=0)

    cos = cos_ref[...]
    sin = sin_ref[...]

    def rot(x):
        return x * cos + pltpu.roll(x, HEAD // 2, 1) * sin

    for h in range(n_heads):
        sl = slice(h * HEAD, (h + 1) * HEAD)
        q = rot(rows(p_ref[0, :, sl]))
        k = rot(rows(p_ref[1, :, sl])) * (HEAD ** -0.5)
        v = rows(p_ref[2, :, sl]).astype(cd)
        g = rows(p_ref[3, :, sl])
        intra = _mm(q.astype(cd), k.astype(cd), NT) * dmask_ref[h]
        o = _mm(intra.astype(cd), v)
        s = s_ref[h]
        o = o + _mm((q * qdec_ref[h]).astype(cd), s.astype(cd))
        kv = _mm((k * kdec_ref[h]).astype(cd), v, TN)
        s_ref[h] = gch[h] * s + kv
        oa = _rms(o) * _silu(g)
        y_ref[:, sl] = oa[:rows_in].astype(y_ref.dtype)

    u = _gelu(rows(p_ref[4]))
    vg = _gelu(rows(p_ref[5]))
    xc = vg - jnp.mean(vg, axis=-1, keepdims=True)
    vb = xc * lax.rsqrt(jnp.mean(xc * xc, axis=-1, keepdims=True) + EPS) * sgg_ref[...]
    if emit_vb:
        vb_ref[...] = vb[:rows_in]
    row = lax.broadcasted_iota(jnp.int32, (C, C), 0)
    col = lax.broadcasted_iota(jnp.int32, (C, C), 1)
    for gi in range(n_groups):
        sl = slice(gi * HEAD, (gi + 1) * HEAD)
        w = jnp.where(col <= row, ws_ref[gi], 0.0).astype(cd)
        f = _mm(w, vb[:, sl].astype(cd)) + sgb_ref[gi]
        ob = u[:, sl] * f
        y_ref[:, w_ret + gi * HEAD:w_ret + (gi + 1) * HEAD] = ob[:rows_in].astype(y_ref.dtype)

    @pl.when(c == pl.num_programs(1) - 1)
    def _():
        sout_ref[...] = s_ref[...]


def even_core(proj, B, L, rows_in, cos, sin, dmask, qdec, kdec, gch, sg_g, w_s, sgb, s0, emit_vb, out_dtype):
    n_heads = dmask.shape[0]
    n_groups = w_s.shape[0]
    W = proj.shape[2]
    nc = L // rows_in
    has_s0 = s0 is not None
    const3 = lambda b, c: (0, 0, 0)
    in_specs = [
        pl.BlockSpec((6, rows_in, W), lambda b, c: (0, b * nc + c, 0)),
        pl.BlockSpec((HEAD, HEAD), lambda b, c: (c, 0)),
        pl.BlockSpec((HEAD, HEAD), lambda b, c: (c, 0)),
        pl.BlockSpec(dmask.shape, const3),
        pl.BlockSpec(qdec.shape, const3),
        pl.BlockSpec(kdec.shape, const3),
        pl.BlockSpec((1, W), lambda b, c: (0, 0)),
        pl.BlockSpec(w_s.shape, const3),
        pl.BlockSpec(sgb.shape, const3),
    ]
    args = [proj, cos, sin, dmask, qdec, kdec, sg_g.reshape(1, W), w_s, sgb]
    if has_s0:
        in_specs.append(pl.BlockSpec((None, n_heads, HEAD, HEAD), lambda b, c: (b, 0, 0, 0)))
        args.append(s0)
    out_specs = [
        pl.BlockSpec((rows_in, 2 * W), lambda b, c: (b * nc + c, 0)),
        pl.BlockSpec((None, n_heads, HEAD, HEAD), lambda b, c: (b, 0, 0, 0)),
    ]
    out_shape = [
        jax.ShapeDtypeStruct((B * L, 2 * W), out_dtype),
        jax.ShapeDtypeStruct((B, n_heads, HEAD, HEAD), F32),
    ]
    if emit_vb:
        out_specs.append(pl.BlockSpec((rows_in, W), lambda b, c: (b * nc + c, 0)))
        out_shape.append(jax.ShapeDtypeStruct((B * L, W), F32))
    body = functools.partial(_even_core_body, n_heads=n_heads, n_groups=n_groups, gch=gch,
                             rows_in=rows_in, has_s0=has_s0, emit_vb=emit_vb)
    return pl.pallas_call(
        body,
        grid=(B, nc),
        in_specs=in_specs,
        out_specs=out_specs,
        out_shape=out_shape,
        scratch_shapes=[pltpu.VMEM((n_heads, HEAD, HEAD), F32)],
        compiler_params=_cparams("parallel", "arbitrary"),
        name="even_core",
    )(*args)


def _mem_kv_body(m_ref, wk_ref, wv_ref, kg_ref, k_ref, v_ref, *, n_heads):
    m = m_ref[...].astype(BF16)
    k = jnp.dot(m, wk_ref[...], preferred_element_type=F32)
    kg = kg_ref[...]
    v = jnp.dot(m, wv_ref[...], preferred_element_type=F32)
    for h in range(n_heads):
        sl = slice(h * HEAD, (h + 1) * HEAD)
        k_ref[:, h, :] = _rms(k[:, sl]) * kg
        v_ref[:, h, :] = v[:, sl]


def mem_kv(mem, w_k, w_v, k_g):
    B, NM, D = mem.shape
    depth, _, WX = w_k.shape
    nh = WX // HEAD
    out = jax.ShapeDtypeStruct((depth, B, NM, nh, HEAD), F32)
    return pl.pallas_call(
        functools.partial(_mem_kv_body, n_heads=WX // HEAD),
        grid=(depth, B),
        in_specs=[
            pl.BlockSpec((None, NM, D), lambda l, b: (b, 0, 0)),
            pl.BlockSpec((None, D, WX), lambda l, b: (l, 0, 0)),
            pl.BlockSpec((None, D, WX), lambda l, b: (l, 0, 0)),
            pl.BlockSpec((None, 1, HEAD), lambda l, b: (l, 0, 0)),
        ],
        out_specs=[pl.BlockSpec((None, None, NM, nh, HEAD), lambda l, b: (l, b, 0, 0, 0))] * 2,
        out_shape=[out, out],
        compiler_params=_cparams("parallel", "parallel"),
        name="mem_kv",
    )(mem, w_k, w_v, k_g.reshape(depth, 1, HEAD))


def _cross_attn_body(x_ref, g_ref, wq_ref, qg_ref, mk_ref, mv_ref, wo_ref, o_ref, *, n_heads):
    cd = wq_ref.dtype
    x = x_ref[...]
    q = _mm((_rms(x) * g_ref[...]).astype(cd), wq_ref[...])
    qg = qg_ref[...]
    outs = []
    for h in range(n_heads):
        sl = slice(h * HEAD, (h + 1) * HEAD)
        qn = (_rms(q[:, sl]) * qg).astype(cd)
        s = _mm(qn, mk_ref[:, h, :].astype(cd), NT) * (HEAD ** -0.5)
        e = jnp.exp(s - jnp.max(s, axis=-1, keepdims=True))
        p = e / jnp.sum(e, axis=-1, keepdims=True)
        outs.append(_mm(p.astype(cd), mv_ref[:, h, :].astype(cd)))
    o = jnp.concatenate(outs, axis=-1).astype(cd)
    o_ref[...] = x + _mm(o, wo_ref[...])


def cross_attn(x, g, wq, qg, mk_all, mv_all, layer, wo, tm):
    B, L, D = x.shape
    NM, nh = mk_all.shape[2], mk_all.shape[3]
    WX = nh * HEAD
    mem_spec = pl.BlockSpec((None, None, NM, nh, HEAD), lambda b, i: (layer, b, 0, 0, 0))
    return pl.pallas_call(
        functools.partial(_cross_attn_body, n_heads=nh),
        grid=(B, L // tm),
        in_specs=[
            pl.BlockSpec((None, tm, D), lambda b, i: (b, i, 0)),
            pl.BlockSpec((1, D), lambda b, i: (0, 0)),
            pl.BlockSpec((D, WX), lambda b, i: (0, 0)),
            pl.BlockSpec((1, HEAD), lambda b, i: (0, 0)),
            mem_spec,
            mem_spec,
            pl.BlockSpec((WX, D), lambda b, i: (0, 0)),
        ],
        out_specs=pl.BlockSpec((None, tm, D), lambda b, i: (b, i, 0)),
        out_shape=jax.ShapeDtypeStruct((B, L, D), F32),
        compiler_params=_cparams("parallel", "parallel"),
        name="cross_attn",
    )(x, g.reshape(1, D), wq, qg.reshape(1, HEAD), mk_all, mv_all, wo)


def _ffn_body(x_ref, g_ref, wg_ref, wu_ref, wd_ref, o_ref, h_ref, acc_ref):
    f = pl.program_id(1)

    @pl.when(f == 0)
    def _():
        h_ref[...] = (_rms(x_ref[...]) * g_ref[...]).astype(h_ref.dtype)
        acc_ref[...] = jnp.zeros_like(acc_ref)

    h = h_ref[...]
    a = _silu(_mm(h, wg_ref[...])) * _mm(h, wu_ref[...])
    acc_ref[...] += _mm(a.astype(h.dtype), wd_ref[...])

    @pl.when(f == pl.num_programs(1) - 1)
    def _():
        o_ref[...] = x_ref[...] + acc_ref[...]


def ffn(x, g, w_gu, w_down, tm, tf):
    M, D = x.shape
    F = w_down.shape[0]
    nf = F // tf
    return pl.pallas_call(
        _ffn_body,
        grid=(M // tm, nf),
        in_specs=[
            pl.BlockSpec((tm, D), lambda i, f: (i, 0)),
            pl.BlockSpec((1, D), lambda i, f: (0, 0)),
            pl.BlockSpec((D, tf), lambda i, f: (0, f)),
            pl.BlockSpec((D, tf), lambda i, f: (0, nf + f)),
            pl.BlockSpec((tf, D), lambda i, f: (f, 0)),
        ],
        out_specs=pl.BlockSpec((tm, D), lambda i, f: (i, 0)),
        out_shape=jax.ShapeDtypeStruct((M, D), F32),
        scratch_shapes=[pltpu.VMEM((tm, D), w_gu.dtype), pltpu.VMEM((tm, D), F32)],
        compiler_params=_cparams("parallel", "arbitrary"),
        name="ffn",
    )(x, g.reshape(1, D), w_gu, w_gu, w_down)


def _router_body(x_ref, g_ref, wr_ref, o_ref, *, n_experts):
    h = _rms(x_ref[...]) * g_ref[...]
    logits = jnp.dot(h, wr_ref[...], preferred_element_type=F32, precision=lax.Precision.HIGHEST)
    lane = lax.broadcasted_iota(jnp.int32, logits.shape, 1).astype(F32)
    neg = jnp.float32(-jnp.inf)
    logits = jnp.where(lane < n_experts, logits, neg)
    m1 = jnp.max(logits, axis=-1, keepdims=True)
    i1 = jnp.min(jnp.where(logits == m1, lane, float(LANES)), axis=-1, keepdims=True)
    rest = jnp.where(lane == i1, neg, logits)
    m2 = jnp.max(rest, axis=-1, keepdims=True)
    i2 = jnp.min(jnp.where(rest == m2, lane, float(LANES)), axis=-1, keepdims=True)
    e2 = jnp.exp(m2 - m1)
    den = 1.0 + e2
    o_ref[...] = jnp.where(lane == i1, 1.0 / den, 0.0) + jnp.where(lane == i2, e2 / den, 0.0)


def router(x, g, w_router_pad, n_experts, tm):
    M, D = x.shape
    return pl.pallas_call(
        functools.partial(_router_body, n_experts=n_experts),
        grid=(M // tm,),
        in_specs=[
            pl.BlockSpec((tm, D), lambda i: (i, 0)),
            pl.BlockSpec((1, D), lambda i: (0, 0)),
            pl.BlockSpec((D, LANES), lambda i: (0, 0)),
        ],
        out_specs=pl.BlockSpec((tm, LANES), lambda i: (i, 0)),
        out_shape=jax.ShapeDtypeStruct((M, LANES), F32),
        compiler_params=_cparams("parallel"),
        name="router",
    )(x, g.reshape(1, D), w_router_pad)


def _moe_body(x_ref, g_ref, gate_ref, wg_ref, wu_ref, wd_ref, o_ref, h_ref, acc_ref):
    e = pl.program_id(1)

    @pl.when(e == 0)
    def _():
        h_ref[...] = (_rms(x_ref[...]) * g_ref[...]).astype(BF16)
        acc_ref[...] = jnp.zeros_like(acc_ref)

    h = h_ref[...]
    a = _silu(jnp.dot(h, wg_ref[...], preferred_element_type=F32)) * jnp.dot(h, wu_ref[...], preferred_element_type=F32)
    acc_ref[...] += gate_ref[...] * jnp.dot(a.astype(BF16), wd_ref[...], preferred_element_type=F32)

    @pl.when(e == pl.num_programs(1) - 1)
    def _():
        o_ref[...] = x_ref[...] + acc_ref[...]


def moe_dense(x, g, gates_t, w_gu, w_down, tm):
    M, D = x.shape
    E, F, _ = w_down.shape
    return pl.pallas_call(
        _moe_body,
        grid=(M // tm, E),
        in_specs=[
            pl.BlockSpec((tm, D), lambda i, e: (i, 0)),
            pl.BlockSpec((1, D), lambda i, e: (0, 0)),
            pl.BlockSpec((None, tm, 1), lambda i, e: (e, i, 0)),
            pl.BlockSpec((None, D, F), lambda i, e: (e, 0, 0)),
            pl.BlockSpec((None, D, F), lambda i, e: (e, 0, 1)),
            pl.BlockSpec((None, F, D), lambda i, e: (e, 0, 0)),
        ],
        out_specs=pl.BlockSpec((tm, D), lambda i, e: (i, 0)),
        out_shape=jax.ShapeDtypeStruct((M, D), F32),
        scratch_shapes=[pltpu.VMEM((tm, D), BF16), pltpu.VMEM((tm, D), F32)],
        compiler_params=_cparams("parallel", "arbitrary"),
        name="moe",
    )(x, g.reshape(1, D), gates_t, w_gu, w_gu, w_down)


def _log_terms(z):
    lp = jnp.log(1.0 + jnp.exp(-jnp.abs(z)))
    lom = -jnp.maximum(z, 0.0) - lp
    return lom, z + lom


def _sb_prompt_body(q_ref, k_ref, v_ref, b_ref, u_ref, o_ref, acc_ref, r_ref, *, tq):
    i = pl.program_id(2)
    q = (q_ref[...] * (HEAD ** -0.5)).astype(BF16)
    bias = b_ref[...]
    acc_ref[...] = jnp.zeros_like(acc_ref)
    r_ref[...] = jnp.zeros_like(r_ref)
    reps = tq // LANES

    def block(kb, masked):
        start = pl.multiple_of(kb * tq, tq)
        k = k_ref[pl.ds(start, tq), :].astype(BF16)
        v = v_ref[pl.ds(start, tq), :].astype(BF16)
        z = lax.dot_general(q, k, NT, preferred_element_type=F32) + bias
        lom, lbeta = _log_terms(z)
        if masked:
            row = lax.broadcasted_iota(jnp.int32, (tq, tq), 0)
            col = lax.broadcasted_iota(jnp.int32, (tq, tq), 1)
            keep = col < row
            lom = jnp.where(keep, lom, 0.0)
        c = jnp.dot(lom.astype(BF16), u_ref[...], preferred_element_type=F32)
        r = r_ref[...]
        a = jnp.exp(lbeta + c[:, :tq] + jnp.concatenate([r] * reps, axis=1))
        if masked:
            a = jnp.where(keep, a, 0.0)
        acc_ref[...] += jnp.dot(a.astype(BF16), v, preferred_element_type=F32)
        r_ref[...] = r + c[:, tq:]

    block(i, True)

    def body(jj, carry):
        block(i - 1 - jj, False)
        return carry

    lax.fori_loop(0, i, body, 0)
    o_ref[...] = acc_ref[...].astype(o_ref.dtype)


def _suffix_matrix(t):
    j = np.arange(t)[:, None]
    s = np.arange(t)[None, :]
    u = np.concatenate([(j > s).astype(np.float32), np.ones((t, LANES), np.float32)], axis=1)
    return jnp.asarray(u, dtype=BF16)


def sb_prompt(proj, bias, B, S, tq):
    W = proj.shape[2]
    H = W // HEAD
    nq = S // tq
    bias_b = jnp.broadcast_to(bias.astype(F32)[:, None, None], (H, 1, tq))
    return pl.pallas_call(
        functools.partial(_sb_prompt_body, tq=tq),
        grid=(B, H, nq),
        in_specs=[
            pl.BlockSpec((None, tq, HEAD), lambda b, h, i: (0, b * nq + i, h)),
            pl.BlockSpec((None, S, HEAD), lambda b, h, i: (1, b, h)),
            pl.BlockSpec((None, S, HEAD), lambda b, h, i: (2, b, h)),
            pl.BlockSpec((None, 1, tq), lambda b, h, i: (h, 0, 0)),
            pl.BlockSpec((tq, tq + LANES), lambda b, h, i: (0, 0)),
        ],
        out_specs=pl.BlockSpec((tq, HEAD), lambda b, h, i: (b * nq + i, h)),
        out_shape=jax.ShapeDtypeStruct((B * S, W), BF16),
        scratch_shapes=[pltpu.VMEM((tq, HEAD), F32), pltpu.VMEM((tq, LANES), F32)],
        compiler_params=_cparams("parallel", "parallel", "arbitrary"),
        name="sb_prompt",
    )(proj, proj, proj, bias_b, _suffix_matrix(tq))


def _sb_sample_body(pt_ref, q_ref, kn_ref, vn_ref, b_ref, u_ref, *rest, n_heads, ppb, t_valid):
    kp_refs = rest[:ppb]
    vp_refs = rest[ppb:2 * ppb]
    o_ref, qx_ref, acc_ref, r_ref = rest[2 * ppb:]
    del pt_ref
    pg = pl.program_id(1)
    G = LANES // n_heads
    bias = b_ref[...]
    col = lax.broadcasted_iota(jnp.int32, (PAGE, LANES), 1)
    row = lax.broadcasted_iota(jnp.int32, (PAGE, LANES), 0)

    def page(get_k, get_v, masked):
        z = bias
        for h in range(n_heads):
            z = z + lax.dot_general(get_k(h), qx_ref[h], NT, preferred_element_type=F32)
        lom, lbeta = _log_terms(z)
        if masked:
            t = col & (G - 1)
            keep = (row < t) & (t < t_valid)
            lom = jnp.where(keep, lom, 0.0)
        after = jnp.dot(u_ref[...], lom.astype(BF16), preferred_element_type=F32)
        r = r_ref[...]
        a = jnp.exp(lbeta + after + r)
        if masked:
            a = jnp.where(keep, a, 0.0)
        a_t = a.T.astype(BF16)
        for h in range(n_heads):
            acc_ref[h] += jnp.dot(a_t[h * G:(h + 1) * G], get_v(h), preferred_element_type=F32)
        r_ref[...] = r + jnp.sum(lom, axis=0, keepdims=True)

    def pad_rows(x, before, total):
        parts = []
        if before:
            parts.append(jnp.zeros((before, x.shape[1]), x.dtype))
        parts.append(x)
        if total - before - x.shape[0]:
            parts.append(jnp.zeros((total - before - x.shape[0], x.shape[1]), x.dtype))
        return jnp.concatenate(parts, axis=0)

    @pl.when(pg == 0)
    def _():
        for h in range(n_heads):
            sl = slice(h * HEAD, (h + 1) * HEAD)
            qx_ref[h] = pad_rows(q_ref[:, sl] * (HEAD ** -0.5), h * G, LANES).astype(BF16)
        acc_ref[...] = jnp.zeros_like(acc_ref)
        r_ref[...] = jnp.zeros_like(r_ref)

        def new_rows(ref):
            return lambda h: pad_rows(ref[:, h * HEAD:(h + 1) * HEAD], 0, PAGE).astype(BF16)

        page(new_rows(kn_ref), new_rows(vn_ref), True)

    for n in range(ppb):
        page(lambda h: kp_refs[n][:, h, :].astype(BF16), lambda h: vp_refs[n][:, h, :].astype(BF16), False)

    @pl.when(pg == pl.num_programs(1) - 1)
    def _():
        for h in range(n_heads):
            o_ref[:, h * HEAD:(h + 1) * HEAD] = acc_ref[h][:T_PAD].astype(o_ref.dtype)


def sb_sample(proj, bias, pool_k, pool_v, layer, page_table, t_valid, ppb, out_dtype):
    W = proj.shape[2]
    H = W // HEAD
    G = LANES // H
    B, n_pages = page_table.shape
    npg = n_pages // ppb
    bias_b = jnp.repeat(bias.astype(F32), G).reshape(1, LANES)
    j = np.arange(PAGE)
    u_t = jnp.asarray((j[None, :] > j[:, None]).astype(np.float32), dtype=BF16)

    def tok(g):
        return pl.BlockSpec((None, T_PAD, W), lambda b, p, pt: (g, b, 0))

    def pool_spec(n):
        return pl.BlockSpec((None, None, PAGE, H, HEAD),
                            lambda b, p, pt: (layer, pt[b, n_pages - 1 - (p * ppb + n)], 0, 0, 0))

    grid_spec = pltpu.PrefetchScalarGridSpec(
        num_scalar_prefetch=1,
        grid=(B, npg),
        in_specs=[tok(0), tok(1), tok(2),
                  pl.BlockSpec((1, LANES), lambda b, p, pt: (0, 0)),
                  pl.BlockSpec((PAGE, PAGE), lambda b, p, pt: (0, 0))]
        + [pool_spec(n) for n in range(ppb)] * 2,
        out_specs=pl.BlockSpec((T_PAD, W), lambda b, p, pt: (b, 0)),
        scratch_shapes=[pltpu.VMEM((H, LANES, HEAD), BF16), pltpu.VMEM((H, G, HEAD), F32),
                        pltpu.VMEM((1, LANES), F32)],
    )
    return pl.pallas_call(
        functools.partial(_sb_sample_body, n_heads=H, ppb=ppb, t_valid=t_valid),
        grid_spec=grid_spec,
        out_shape=jax.ShapeDtypeStruct((B * T_PAD, W), out_dtype),
        compiler_params=_cparams("parallel", "arbitrary"),
        name="sb_sample",
    )(page_table, proj, proj, proj, bias_b, u_t, *([pool_k] * ppb), *([pool_v] * ppb))


def _rope_tables(pos):
    half = HEAD // 2
    inv = 1.0 / (ROPE_BASE ** jnp.linspace(0.0, 1.0, half, dtype=F32))
    ang = pos[:, None] * inv[None, :]
    cos, sin = jnp.cos(ang), jnp.sin(ang)
    return jnp.concatenate([cos, cos], axis=-1), jnp.concatenate([-sin, sin], axis=-1)


def _retention_tables(n_heads, chunk):
    log_g = np.log(1.0 - 2.0 ** (-5.0 - np.arange(n_heads, dtype=np.float64)))
    idx = np.arange(HEAD, dtype=np.float64)
    diff = idx[:, None] - idx[None, :]
    dmask = np.where(diff >= 0, np.exp(log_g[:, None, None] * np.maximum(diff, 0.0)), 0.0)
    qdec = np.exp(log_g[:, None] * (idx + 1.0)[None, :])
    kdec = np.where(idx[None, :] < chunk, np.exp(log_g[:, None] * (chunk - 1.0 - idx)[None, :]), 0.0)
    bc = lambda t: jnp.asarray(np.broadcast_to(t[:, :, None], (n_heads, HEAD, HEAD)), dtype=F32)
    gch = tuple(float(x) for x in np.exp(log_g * chunk))
    return jnp.asarray(dmask, dtype=F32), bc(qdec), bc(kdec), gch


def kernel(x_prompt, x_sample, mem_prompt, state_ret, cache_sb_k, cache_sb_v, page_table, cache_mem_k, cache_mem_v, norm_mix_g, norm_x_g, norm_ffn_g, w_in_even, w_out_even, sg_norm_g, sg_w_s, sg_b, w_qkv_sb, w_o_sb, sb_bias, w_q_x, w_k_x, w_v_x, w_o_x, q_norm_x_g, k_norm_x_g, w_gu_dense, w_down_dense, w_router, w_gu_moe, w_down_moe):
    B, S, D = x_prompt.shape
    BS, T, _ = x_sample.shape
    depth = norm_mix_g.shape[0]
    n_heads_ret = state_ret.shape[2]
    n_experts = w_router.shape[2]
    past = page_table.shape[1] * PAGE
    H_sb = cache_sb_k.shape[3]
    W_sb = H_sb * HEAD
    W_x = w_q_x.shape[2]

    bf = lambda w: w.astype(BF16)
    TM = min(1024, B * S)
    MS = BS * T_PAD

    xp = x_prompt.reshape(B * S, D)
    xs = jnp.pad(x_sample, ((0, 0), (0, T_PAD - T), (0, 0))).reshape(MS, D)

    cos_p, sin_p = _rope_tables(jnp.arange(S, dtype=F32))
    cos_s, sin_s = _rope_tables(past + jnp.arange(HEAD, dtype=F32))
    dmask, qdec, kdec_p, gch_p = _retention_tables(n_heads_ret, HEAD)
    _, _, kdec_s, gch_s = _retention_tables(n_heads_ret, T)

    mk_p, mv_p = mem_kv(mem_prompt, bf(w_k_x), bf(w_v_x), k_norm_x_g)

    ret_p, ret_s, sgv_s = [], [], []
    sbk_p, sbv_p, sbk_s, sbv_s = [], [], [], []
    for l in range(depth):
        i = l // 2
        sample_f32 = l + 2 < depth
        if l % 2 == 0:
            w_in, w_out = bf(w_in_even[i]), bf(w_out_even[i])
            sgb = jnp.broadcast_to(sg_b[i][:, :, None], sg_b[i].shape + (HEAD,))
            proj = norm_mm(xp, norm_mix_g[l], w_in, 6, TM, 512)
            y, sp = even_core(proj, B, S, HEAD, cos_p, sin_p, dmask, qdec, kdec_p, gch_p,
                              sg_norm_g[i], sg_w_s[i], sgb, None, False, BF16)
            xp = mm_res(y, w_out, xp, TM, 512)
            if sample_f32:
                w_in, w_out = w_in_even[i], w_out_even[i]
            proj = norm_mm(xs, norm_mix_g[l], w_in, 6, MS, 512)
            y, ss, vs = even_core(proj, BS, T_PAD, T_PAD, cos_s, sin_s, dmask, qdec, kdec_s, gch_s,
                                  sg_norm_g[i], sg_w_s[i], sgb, state_ret[i], True, w_in.dtype)
            xs = mm_res(y, w_out, xs, MS, 512)
            ret_p.append(sp)
            ret_s.append(ss)
            sgv_s.append(vs.reshape(BS, T_PAD, -1)[:, :T])
        else:
            w_qkv, w_o = bf(w_qkv_sb[i]), bf(w_o_sb[i])
            proj = norm_mm(xp, norm_mix_g[l], w_qkv, 3, TM, 512)
            o = sb_prompt(proj, sb_bias[i], B, S, 256)
            xp = mm_res(o, w_o, xp, TM, 512)
            sbk_p.append(proj[1].reshape(B, S, H_sb, HEAD))
            sbv_p.append(proj[2].reshape(B, S, H_sb, HEAD))
            if sample_f32:
                w_qkv, w_o = w_qkv_sb[i], w_o_sb[i]
            proj = norm_mm(xs, norm_mix_g[l], w_qkv, 3, MS, 512)
            o = sb_sample(proj, sb_bias[i], cache_sb_k, cache_sb_v, i, page_table, T, 4, w_o.dtype)
            xs = mm_res(o, w_o, xs, MS, 512)
            sbk_s.append(proj[1].reshape(BS, T_PAD, H_sb, HEAD)[:, :T])
            sbv_s.append(proj[2].reshape(BS, T_PAD, H_sb, HEAD)[:, :T])
        wq, wo = bf(w_q_x[l]), bf(w_o_x[l])
        xp = cross_attn(xp.reshape(B, S, D), norm_x_g[l], wq, q_norm_x_g[l], mk_p, mv_p, l, wo, 512).reshape(B * S, D)
        if sample_f32:
            wq, wo = w_q_x[l], w_o_x[l]
        xs = cross_attn(xs.reshape(BS, T_PAD, D), norm_x_g[l], wq, q_norm_x_g[l], cache_mem_k, cache_mem_v, l, wo, T_PAD).reshape(MS, D)
        if l % 2 == 0:
            w_gu, w_down = bf(w_gu_dense[i]), bf(w_down_dense[i])
            xp = ffn(xp, norm_ffn_g[l], w_gu, w_down, TM, 256)
            if sample_f32:
                w_gu, w_down = w_gu_dense[i], w_down_dense[i]
            xs = ffn(xs, norm_ffn_g[l], w_gu, w_down, MS, 256)
        else:
            w_gu, w_down = bf(w_gu_moe[i]), bf(w_down_moe[i])
            wr = jnp.pad(w_router[i], ((0, 0), (0, LANES - n_experts)))

            def moe(x, tm):
                gates = router(x, norm_ffn_g[l], wr, n_experts, tm)
                gates_t = jnp.transpose(gates[:, :n_experts])[:, :, None]
                return moe_dense(x, norm_ffn_g[l], gates_t, w_gu, w_down, tm)

            xp, xs = moe(xp, min(512, B * S)), moe(xs, MS)

    y_p = xp.reshape(B, S, D)
    y_s = xs.reshape(BS, T_PAD, D)[:, :T]
    return (y_p, y_s, jnp.stack(ret_p), jnp.stack(ret_s), jnp.stack(sgv_s),
            jnp.stack(sbk_p), jnp.stack(sbv_p), jnp.stack(sbk_s), jnp.stack(sbv_s), mk_p, mv_p)
```

```python
import functools
import math

import numpy as np
import jax
import jax.numpy as jnp
from jax import lax
from jax.experimental import pallas as pl
from jax.experimental.pallas import tpu as pltpu

F32 = jnp.float32
BF16 = jnp.bfloat16
EPS = 1e-6
ROPE_BASE = 10000.0
LANES = 128
VMEM_LIMIT_BYTES = 56 * 1024 * 1024
HEAD = 128
PAGE = 128
T_PAD = 8
NN = (((1,), (0,)), ((), ()))
NT = (((1,), (1,)), ((), ()))
TN = (((0,), (0,)), ((), ()))


def _cparams(*sem):
    return pltpu.CompilerParams(dimension_semantics=sem, vmem_limit_bytes=VMEM_LIMIT_BYTES)


def _rms(x):
    return x * lax.rsqrt(jnp.mean(x * x, axis=-1, keepdims=True) + EPS)


def _gelu(x):
    return 0.5 * x * (1.0 + lax.erf(x * (2.0 ** -0.5)))


def _silu(x):
    return x * jax.nn.sigmoid(x)


def _mm(a, b, dims=NN):
    assert a.dtype == b.dtype
    precision = lax.Precision.HIGHEST if a.dtype == F32 else None
    return lax.dot_general(a, b, dims, precision=precision, preferred_element_type=F32)


def _norm_mm_body(x_ref, g_ref, w_ref, o_ref, *rest, first_group_scale, tiles_per_group):
    h_ref = rest[-1]
    j = pl.program_id(1)

    @pl.when(j == 0)
    def _():
        h_ref[...] = (_rms(x_ref[...]) * g_ref[...]).astype(h_ref.dtype)

    o = _mm(h_ref[...], w_ref[...])
    o_ref[...] = o
    if first_group_scale is not None:
        scale = jnp.where(j < tiles_per_group, first_group_scale, 1.0)
        rest[0][...] = (o * scale).astype(BF16)


def norm_mm(x, g, w, groups, tm, tn, first_group_scale=None):
    M, K = x.shape
    N = w.shape[1]
    ng = N // groups
    npg = ng // tn
    out_spec = pl.BlockSpec((None, tm, tn), lambda i, j: (j // npg, i, j % npg))
    out_shape = jax.ShapeDtypeStruct((groups, M, ng), F32)
    with_copy = first_group_scale is not None
    return pl.pallas_call(
        functools.partial(_norm_mm_body, first_group_scale=first_group_scale, tiles_per_group=npg),
        grid=(M // tm, N // tn),
        in_specs=[
            pl.BlockSpec((tm, K), lambda i, j: (i, 0)),
            pl.BlockSpec((1, K), lambda i, j: (0, 0)),
            pl.BlockSpec((K, tn), lambda i, j: (0, j)),
        ],
        out_specs=[out_spec, out_spec] if with_copy else out_spec,
        out_shape=[out_shape, jax.ShapeDtypeStruct((groups, M, ng), BF16)] if with_copy else out_shape,
        scratch_shapes=[pltpu.VMEM((tm, K), w.dtype)],
        compiler_params=_cparams("parallel", "arbitrary"),
        name="norm_mm",
    )(x, g.reshape(1, K), w)


def _mm_res_body(a_ref, w_ref, r_ref, o_ref):
    o_ref[...] = r_ref[...] + _mm(a_ref[...], w_ref[...])


def mm_res(a, w, res, tm, tn):
    M, K = a.shape
    N = w.shape[1]
    tn = min(tn, N)
    return pl.pallas_call(
        _mm_res_body,
        grid=(M // tm, N // tn),
        in_specs=[
            pl.BlockSpec((tm, K), lambda i, j: (i, 0)),
            pl.BlockSpec((K, tn), lambda i, j: (0, j)),
            pl.BlockSpec((tm, tn), lambda i, j: (i, j)),
        ],
        out_specs=pl.BlockSpec((tm, tn), lambda i, j: (i, j)),
        out_shape=jax.ShapeDtypeStruct((M, N), F32),
        compiler_params=_cparams("parallel", "arbitrary"),
        name="mm_res",
    )(a, w, res)


def _even_core_body(*refs, n_heads, n_groups, gch, rows_in, has_s0, emit_vb):
    it = iter(refs)
    p_ref, cos_ref, sin_ref, dmask_ref, qdec_ref, kdec_ref, sgg_ref, ws_ref, sgb_ref = (next(it) for _ in range(9))
    s0_ref = next(it) if has_s0 else None
    y_ref, sout_ref = next(it), next(it)
    vb_ref = next(it) if emit_vb else None
    s_ref = next(it)
    C = HEAD
    cd = y_ref.dtype
    c = pl.program_id(1)
    w_ret = n_heads * HEAD

    @pl.when(c == 0)
    def _():
        if has_s0:
            s_ref[...] = s0_ref[...]
        else:
            s_ref[...] = jnp.zeros_like(s_ref)

    def rows(x):
        if rows_in == C:
            return x
        return jnp.concatenate([x, jnp.zeros((C - rows_in, x.shape[1]), x.dtype)], axis=0)

    cos = cos_ref[...]
    sin = sin_ref[...]

    def rot(x):
        return x * cos + pltpu.roll(x, HEAD // 2, 1) * sin

    for h in range(n_heads):
        sl = slice(h * HEAD, (h + 1) * HEAD)
        q = rot(rows(p_ref[0, :, sl]))
        k = rot(rows(p_ref[1, :, sl])) * (HEAD ** -0.5)
        v = rows(p_ref[2, :, sl]).astype(cd)
        g = rows(p_ref[3, :, sl])
        intra = _mm(q.astype(cd), k.astype(cd), NT) * dmask_ref[h]
        o = _mm(intra.astype(cd), v)
        s = s_ref[h]
        o = o + _mm((q * qdec_ref[h]).astype(cd), s.astype(cd))
        kv = _mm((k * kdec_ref[h]).astype(cd), v, TN)
        s_ref[h] = gch[h] * s + kv
        oa = _rms(o) * _silu(g)
        y_ref[:, sl] = oa[:rows_in].astype(y_ref.dtype)

    u = _gelu(rows(p_ref[4]))
    vg = _gelu(rows(p_ref[5]))
    xc = vg - jnp.mean(vg, axis=-1, keepdims=True)
    vb = xc * lax.rsqrt(jnp.mean(xc * xc, axis=-1, keepdims=True) + EPS) * sgg_ref[...]
    if emit_vb:
        vb_ref[...] = vb[:rows_in]
    row = lax.broadcasted_iota(jnp.int32, (C, C), 0)
    col = lax.broadcasted_iota(jnp.int32, (C, C), 1)
    for gi in range(n_groups):
        sl = slice(gi * HEAD, (gi + 1) * HEAD)
        w = jnp.where(col <= row, ws_ref[gi], 0.0).astype(cd)
        f = _mm(w, vb[:, sl].astype(cd)) + sgb_ref[gi]
        ob = u[:, sl] * f
        y_ref[:, w_ret + gi * HEAD:w_ret + (gi + 1) * HEAD] = ob[:rows_in].astype(y_ref.dtype)

    @pl.when(c == pl.num_programs(1) - 1)
    def _():
        sout_ref[...] = s_ref[...]


def even_core(proj, B, L, rows_in, cos, sin, dmask, qdec, kdec, gch, sg_g, w_s, sgb, s0, emit_vb, out_dtype):
    n_heads = dmask.shape[0]
    n_groups = w_s.shape[0]
    W = proj.shape[2]
    nc = L // rows_in
    has_s0 = s0 is not None
    const3 = lambda b, c: (0, 0, 0)
    in_specs = [
        pl.BlockSpec((6, rows_in, W), lambda b, c: (0, b * nc + c, 0)),
        pl.BlockSpec((HEAD, HEAD), lambda b, c: (c, 0)),
        pl.BlockSpec((HEAD, HEAD), lambda b, c: (c, 0)),
        pl.BlockSpec(dmask.shape, const3),
        pl.BlockSpec(qdec.shape, const3),
        pl.BlockSpec(kdec.shape, const3),
        pl.BlockSpec((1, W), lambda b, c: (0, 0)),
        pl.BlockSpec(w_s.shape, const3),
        pl.BlockSpec(sgb.shape, const3),
    ]
    args = [proj, cos, sin, dmask, qdec, kdec, sg_g.reshape(1, W), w_s, sgb]
    if has_s0:
        in_specs.append(pl.BlockSpec((None, n_heads, HEAD, HEAD), lambda b, c: (b, 0, 0, 0)))
        args.append(s0)
    out_specs = [
        pl.BlockSpec((rows_in, 2 * W), lambda b, c: (b * nc + c, 0)),
        pl.BlockSpec((None, n_heads, HEAD, HEAD), lambda b, c: (b, 0, 0, 0)),
    ]
    out_shape = [
        jax.ShapeDtypeStruct((B * L, 2 * W), out_dtype),
        jax.ShapeDtypeStruct((B, n_heads, HEAD, HEAD), F32),
    ]
    if emit_vb:
        out_specs.append(pl.BlockSpec((rows_in, W), lambda b, c: (b * nc + c, 0)))
        out_shape.append(jax.ShapeDtypeStruct((B * L, W), F32))
    body = functools.partial(_even_core_body, n_heads=n_heads, n_groups=n_groups, gch=gch,
                             rows_in=rows_in, has_s0=has_s0, emit_vb=emit_vb)
    return pl.pallas_call(
        body,
        grid=(B, nc),
        in_specs=in_specs,
        out_specs=out_specs,
        out_shape=out_shape,
        scratch_shapes=[pltpu.VMEM((n_heads, HEAD, HEAD), F32)],
        compiler_params=_cparams("parallel", "arbitrary"),
        name="even_core",
    )(*args)


def _mem_kv_body(m_ref, wk_ref, wv_ref, kg_ref, k_ref, v_ref, *, n_heads):
    m = m_ref[...].astype(BF16)
    k = jnp.dot(m, wk_ref[...], preferred_element_type=F32)
    kg = kg_ref[...]
    v = jnp.dot(m, wv_ref[...], preferred_element_type=F32)
    for h in range(n_heads):
        sl = slice(h * HEAD, (h + 1) * HEAD)
        k_ref[:, h, :] = _rms(k[:, sl]) * kg
        v_ref[:, h, :] = v[:, sl]


def mem_kv(mem, w_k, w_v, k_g):
    B, NM, D = mem.shape
    depth, _, WX = w_k.shape
    nh = WX // HEAD
    out = jax.ShapeDtypeStruct((depth, B, NM, nh, HEAD), F32)
    return pl.pallas_call(
        functools.partial(_mem_kv_body, n_heads=WX // HEAD),
        grid=(depth, B),
        in_specs=[
            pl.BlockSpec((None, NM, D), lambda l, b: (b, 0, 0)),
            pl.BlockSpec((None, D, WX), lambda l, b: (l, 0, 0)),
            pl.BlockSpec((None, D, WX), lambda l, b: (l, 0, 0)),
            pl.BlockSpec((None, 1, HEAD), lambda l, b: (l, 0, 0)),
        ],
        out_specs=[pl.BlockSpec((None, None, NM, nh, HEAD), lambda l, b: (l, b, 0, 0, 0))] * 2,
        out_shape=[out, out],
        compiler_params=_cparams("parallel", "parallel"),
        name="mem_kv",
    )(mem, w_k, w_v, k_g.reshape(depth, 1, HEAD))


def _cross_attn_body(x_ref, g_ref, wq_ref, qg_ref, mk_ref, mv_ref, wo_ref, o_ref, *, n_heads):
    cd = wq_ref.dtype
    x = x_ref[...]
    q = _mm((_rms(x) * g_ref[...]).astype(cd), wq_ref[...])
    qg = qg_ref[...]
    outs = []
    for h in range(n_heads):
        sl = slice(h * HEAD, (h + 1) * HEAD)
        qn = (_rms(q[:, sl]) * qg).astype(cd)
        s = _mm(qn, mk_ref[:, h, :].astype(cd), NT) * (HEAD ** -0.5)
        e = jnp.exp(s - jnp.max(s, axis=-1, keepdims=True))
        p = e / jnp.sum(e, axis=-1, keepdims=True)
        outs.append(_mm(p.astype(cd), mv_ref[:, h, :].astype(cd)))
    o = jnp.concatenate(outs, axis=-1).astype(cd)
    o_ref[...] = x + _mm(o, wo_ref[...])


def cross_attn(x, g, wq, qg, mk_all, mv_all, layer, wo, tm):
    B, L, D = x.shape
    NM, nh = mk_all.shape[2], mk_all.shape[3]
    WX = nh * HEAD
    mem_spec = pl.BlockSpec((None, None, NM, nh, HEAD), lambda b, i: (layer, b, 0, 0, 0))
    return pl.pallas_call(
        functools.partial(_cross_attn_body, n_heads=nh),
        grid=(B, L // tm),
        in_specs=[
            pl.BlockSpec((None, tm, D), lambda b, i: (b, i, 0)),
            pl.BlockSpec((1, D), lambda b, i: (0, 0)),
            pl.BlockSpec((D, WX), lambda b, i: (0, 0)),
            pl.BlockSpec((1, HEAD), lambda b, i: (0, 0)),
            mem_spec,
            mem_spec,
            pl.BlockSpec((WX, D), lambda b, i: (0, 0)),
        ],
        out_specs=pl.BlockSpec((None, tm, D), lambda b, i: (b, i, 0)),
        out_shape=jax.ShapeDtypeStruct((B, L, D), F32),
        compiler_params=_cparams("parallel", "parallel"),
        name="cross_attn",
    )(x, g.reshape(1, D), wq, qg.reshape(1, HEAD), mk_all, mv_all, wo)


def _ffn_body(x_ref, g_ref, wg_ref, wu_ref, wd_ref, o_ref, h_ref, acc_ref):
    f = pl.program_id(1)

    @pl.when(f == 0)
    def _():
        h_ref[...] = (_rms(x_ref[...]) * g_ref[...]).astype(h_ref.dtype)
        acc_ref[...] = jnp.zeros_like(acc_ref)

    h = h_ref[...]
    a = _silu(_mm(h, wg_ref[...])) * _mm(h, wu_ref[...])
    acc_ref[...] += _mm(a.astype(h.dtype), wd_ref[...])

    @pl.when(f == pl.num_programs(1) - 1)
    def _():
        o_ref[...] = x_ref[...] + acc_ref[...]


def ffn(x, g, w_gu, w_down, tm, tf):
    M, D = x.shape
    F = w_down.shape[0]
    nf = F // tf
    return pl.pallas_call(
        _ffn_body,
        grid=(M // tm, nf),
        in_specs=[
            pl.BlockSpec((tm, D), lambda i, f: (i, 0)),
            pl.BlockSpec((1, D), lambda i, f: (0, 0)),
            pl.BlockSpec((D, tf), lambda i, f: (0, f)),
            pl.BlockSpec((D, tf), lambda i, f: (0, nf + f)),
            pl.BlockSpec((tf, D), lambda i, f: (f, 0)),
        ],
        out_specs=pl.BlockSpec((tm, D), lambda i, f: (i, 0)),
        out_shape=jax.ShapeDtypeStruct((M, D), F32),
        scratch_shapes=[pltpu.VMEM((tm, D), w_gu.dtype), pltpu.VMEM((tm, D), F32)],
        compiler_params=_cparams("parallel", "arbitrary"),
        name="ffn",
    )(x, g.reshape(1, D), w_gu, w_gu, w_down)


def _router_body(x_ref, g_ref, wr_ref, o_ref, *, n_experts):
    h = _rms(x_ref[...]) * g_ref[...]
    logits = jnp.dot(h, wr_ref[...], preferred_element_type=F32, precision=lax.Precision.HIGHEST)
    lane = lax.broadcasted_iota(jnp.int32, logits.shape, 1).astype(F32)
    neg = jnp.float32(-jnp.inf)
    logits = jnp.where(lane < n_experts, logits, neg)
    m1 = jnp.max(logits, axis=-1, keepdims=True)
    i1 = jnp.min(jnp.where(logits == m1, lane, float(LANES)), axis=-1, keepdims=True)
    rest = jnp.where(lane == i1, neg, logits)
    m2 = jnp.max(rest, axis=-1, keepdims=True)
    i2 = jnp.min(jnp.where(rest == m2, lane, float(LANES)), axis=-1, keepdims=True)
    e2 = jnp.exp(m2 - m1)
    den = 1.0 + e2
    o_ref[...] = jnp.where(lane == i1, 1.0 / den, 0.0) + jnp.where(lane == i2, e2 / den, 0.0)


def router(x, g, w_router_pad, n_experts, tm):
    M, D = x.shape
    return pl.pallas_call(
        functools.partial(_router_body, n_experts=n_experts),
        grid=(M // tm,),
        in_specs=[
            pl.BlockSpec((tm, D), lambda i: (i, 0)),
            pl.BlockSpec((1, D), lambda i: (0, 0)),
            pl.BlockSpec((D, LANES), lambda i: (0, 0)),
        ],
        out_specs=pl.BlockSpec((tm, LANES), lambda i: (i, 0)),
        out_shape=jax.ShapeDtypeStruct((M, LANES), F32),
        compiler_params=_cparams("parallel"),
        name="router",
    )(x, g.reshape(1, D), w_router_pad)


def _moe_body(x_ref, g_ref, gate_ref, wg_ref, wu_ref, wd_ref, o_ref, h_ref, acc_ref):
    e = pl.program_id(1)

    @pl.when(e == 0)
    def _():
        h_ref[...] = (_rms(x_ref[...]) * g_ref[...]).astype(BF16)
        acc_ref[...] = jnp.zeros_like(acc_ref)

    h = h_ref[...]
    a = _silu(jnp.dot(h, wg_ref[...], preferred_element_type=F32)) * jnp.dot(h, wu_ref[...], preferred_element_type=F32)
    acc_ref[...] += gate_ref[...] * jnp.dot(a.astype(BF16), wd_ref[...], preferred_element_type=F32)

    @pl.when(e == pl.num_programs(1) - 1)
    def _():
        o_ref[...] = x_ref[...] + acc_ref[...]


def moe_dense(x, g, gates_t, w_gu, w_down, tm):
    M, D = x.shape
    E, F, _ = w_down.shape
    return pl.pallas_call(
        _moe_body,
        grid=(M // tm, E),
        in_specs=[
            pl.BlockSpec((tm, D), lambda i, e: (i, 0)),
            pl.BlockSpec((1, D), lambda i, e: (0, 0)),
            pl.BlockSpec((None, tm, 1), lambda i, e: (e, i, 0)),
            pl.BlockSpec((None, D, F), lambda i, e: (e, 0, 0)),
            pl.BlockSpec((None, D, F), lambda i, e: (e, 0, 1)),
            pl.BlockSpec((None, F, D), lambda i, e: (e, 0, 0)),
        ],
        out_specs=pl.BlockSpec((tm, D), lambda i, e: (i, 0)),
        out_shape=jax.ShapeDtypeStruct((M, D), F32),
        scratch_shapes=[pltpu.VMEM((tm, D), BF16), pltpu.VMEM((tm, D), F32)],
        compiler_params=_cparams("parallel", "arbitrary"),
        name="moe",
    )(x, g.reshape(1, D), gates_t, w_gu, w_gu, w_down)


LOG2E = math.log2(math.e)


def _log2_terms(z2):
    sign_bit = jnp.int32(-2 ** 31)
    neg_abs = lax.bitcast_convert_type(lax.bitcast_convert_type(z2, jnp.int32) | sign_bit, F32)
    nlom = jnp.maximum(z2, 0.0) + jnp.log2(1.0 + jnp.exp2(neg_abs))
    return nlom, z2 - nlom


MASKED_LOG2 = -1e30


def _sb_prompt_body(q_ref, k_ref, v_ref, b_ref, u_ref, o_ref, acc_ref, r_ref, lom_ref, lb_ref, *, tk):
    i = pl.program_id(2)
    tq = 2 * tk
    n = 2 * i + 2
    bias = b_ref[...]
    acc_ref[...] = jnp.zeros_like(acc_ref)
    r_ref[...] = jnp.zeros_like(r_ref)

    def keys(ref, kb):
        return ref[pl.ds(pl.multiple_of(kb * tk, tk), tk), :]

    def stage_a(kb, slot, masked):
        z = lax.dot_general(q_ref[...], keys(k_ref, kb), NT, preferred_element_type=F32) + bias
        nlom, lbeta = _log2_terms(z)
        if masked:
            row = lax.broadcasted_iota(jnp.int32, (tq, tk), 0)
            col = lax.broadcasted_iota(jnp.int32, (tq, tk), 1)
            keep = col + kb * tk < row + i * tq
            nlom = jnp.where(keep, nlom, 0.0)
            lbeta = jnp.where(keep, lbeta, MASKED_LOG2)
        lom_ref[slot] = nlom.astype(BF16)
        lb_ref[slot] = lbeta

    def stage_b(kb, slot):
        c = jnp.dot(lom_ref[slot], u_ref[...], preferred_element_type=F32)
        r = r_ref[...]
        a = jnp.exp2(lb_ref[slot] - c[:, :tk] - jnp.concatenate([r] * (tk // LANES), axis=1))
        acc_ref[...] += jnp.dot(a.astype(BF16), keys(v_ref, kb), preferred_element_type=F32)
        r_ref[...] = r + c[:, tk:]

    stage_a(n - 1, 0, True)
    stage_a(n - 2, 1, True)
    stage_b(n - 1, 0)

    def body(p, carry):
        kb = n - 1 - 2 * p
        stage_a(kb, 0, False)
        stage_b(kb + 1, 1)
        stage_a(kb - 1, 1, False)
        stage_b(kb, 0)
        return carry

    lax.fori_loop(1, i + 1, body, 0)
    stage_b(0, 1)
    o_ref[...] = acc_ref[...].astype(o_ref.dtype)


def _suffix_matrix(t):
    j = np.arange(t)[:, None]
    s = np.arange(t)[None, :]
    u = np.concatenate([(j > s).astype(np.float32), np.ones((t, LANES), np.float32)], axis=1)
    return jnp.asarray(u, dtype=BF16)


def sb_prompt(qkv, bias, B, S, tk):
    W = qkv.shape[2]
    H = W // HEAD
    tq = 2 * tk
    nq = S // tq
    bias_b = jnp.broadcast_to((bias.astype(F32) * LOG2E)[:, None, None], (H, 1, tk))
    return pl.pallas_call(
        functools.partial(_sb_prompt_body, tk=tk),
        grid=(B, H, nq),
        in_specs=[
            pl.BlockSpec((None, tq, HEAD), lambda b, h, i: (0, b * nq + i, h)),
            pl.BlockSpec((None, S, HEAD), lambda b, h, i: (1, b, h)),
            pl.BlockSpec((None, S, HEAD), lambda b, h, i: (2, b, h)),
            pl.BlockSpec((None, 1, tk), lambda b, h, i: (h, 0, 0)),
            pl.BlockSpec((tk, tk + LANES), lambda b, h, i: (0, 0)),
        ],
        out_specs=pl.BlockSpec((tq, HEAD), lambda b, h, i: (b * nq + i, h)),
        out_shape=jax.ShapeDtypeStruct((B * S, W), BF16),
        scratch_shapes=[pltpu.VMEM((tq, HEAD), F32), pltpu.VMEM((tq, LANES), F32),
                        pltpu.VMEM((2, tq, tk), BF16), pltpu.VMEM((2, tq, tk), F32)],
        compiler_params=_cparams("parallel", "parallel", "arbitrary"),
        name="sb_prompt",
    )(qkv, qkv, qkv, bias_b, _suffix_matrix(tk))


def _sb_sample_body(pt_ref, q_ref, kn_ref, vn_ref, b_ref, u_ref, *rest, n_heads, ppb, t_valid):
    kp_refs = rest[:ppb]
    vp_refs = rest[ppb:2 * ppb]
    o_ref, qx_ref, acc_ref, r_ref = rest[2 * ppb:]
    del pt_ref
    pg = pl.program_id(1)
    G = LANES // n_heads
    bias = b_ref[...]
    col = lax.broadcasted_iota(jnp.int32, (PAGE, LANES), 1)
    row = lax.broadcasted_iota(jnp.int32, (PAGE, LANES), 0)

    def pages(get_k, get_v, npages, masked):
        z = bias
        for h in range(n_heads):
            k_h = jnp.concatenate([get_k(n, h) for n in range(npages)], axis=0)
            z = z + lax.dot_general(k_h, qx_ref[h], NT, preferred_element_type=F32)
        nlom, lbeta = _log2_terms(z)
        if masked:
            t = col & (G - 1)
            keep = (row < t) & (t < t_valid)
            nlom = jnp.where(keep, nlom, 0.0)
        nlom_p = [nlom[n * PAGE:(n + 1) * PAGE] for n in range(npages)]
        after = jnp.dot(u_ref[...], jnp.concatenate(nlom_p, axis=1).astype(BF16), preferred_element_type=F32)
        r = r_ref[...]
        a_t = []
        for n in range(npages):
            a = jnp.exp2(lbeta[n * PAGE:(n + 1) * PAGE] - after[:, n * LANES:(n + 1) * LANES] - r)
            if masked:
                a = jnp.where(keep, a, 0.0)
            a_t.append(a.T.astype(BF16))
            r = r + jnp.sum(nlom_p[n], axis=0, keepdims=True)
        r_ref[...] = r
        a_t = jnp.concatenate(a_t, axis=1)
        for h in range(n_heads):
            v_h = jnp.concatenate([get_v(n, h) for n in range(npages)], axis=0)
            acc_ref[h] += jnp.dot(a_t[h * G:(h + 1) * G], v_h, preferred_element_type=F32)

    def pad_rows(x, before, total):
        parts = []
        if before:
            parts.append(jnp.zeros((before, x.shape[1]), x.dtype))
        parts.append(x)
        if total - before - x.shape[0]:
            parts.append(jnp.zeros((total - before - x.shape[0], x.shape[1]), x.dtype))
        return jnp.concatenate(parts, axis=0)

    @pl.when(pg == 0)
    def _():
        for h in range(n_heads):
            sl = slice(h * HEAD, (h + 1) * HEAD)
            qx_ref[h] = pad_rows(q_ref[:, sl] * (LOG2E * HEAD ** -0.5), h * G, LANES).astype(BF16)
        acc_ref[...] = jnp.zeros_like(acc_ref)
        r_ref[...] = jnp.zeros_like(r_ref)

        def new_rows(ref):
            return lambda n, h: pad_rows(ref[:, h * HEAD:(h + 1) * HEAD], 0, PAGE).astype(BF16)

        pages(new_rows(kn_ref), new_rows(vn_ref), 1, True)

    def pool_rows(refs):
        return lambda n, h: refs[n][pl.ds(h, PAGE, stride=n_heads), :].astype(BF16)

    pages(pool_rows(kp_refs), pool_rows(vp_refs), ppb, False)

    @pl.when(pg == pl.num_programs(1) - 1)
    def _():
        for h in range(n_heads):
            o_ref[:, h * HEAD:(h + 1) * HEAD] = acc_ref[h][:T_PAD].astype(o_ref.dtype)


def sb_sample(proj, bias, pool_k, pool_v, layer, page_table, t_valid, ppb, out_dtype):
    W = proj.shape[2]
    H = W // HEAD
    G = LANES // H
    B, n_pages = page_table.shape
    npg = n_pages // ppb
    bias_b = jnp.repeat(bias.astype(F32) * LOG2E, G).reshape(1, LANES)
    j = np.arange(PAGE)
    u_t = jnp.asarray((j[None, :] > j[:, None]).astype(np.float32), dtype=BF16)

    def rows_view(pool):
        return pool.reshape(pool.shape[0], pool.shape[1], PAGE * H, HEAD)

    def tok(g):
        return pl.BlockSpec((None, T_PAD, W), lambda b, p, pt: (g, b, 0))

    def pool_spec(n):
        return pl.BlockSpec((None, None, PAGE * H, HEAD),
                            lambda b, p, pt: (layer, pt[b, n_pages - 1 - (p * ppb + n)], 0, 0))

    grid_spec = pltpu.PrefetchScalarGridSpec(
        num_scalar_prefetch=1,
        grid=(B, npg),
        in_specs=[tok(0), tok(1), tok(2),
                  pl.BlockSpec((1, LANES), lambda b, p, pt: (0, 0)),
                  pl.BlockSpec((PAGE, PAGE), lambda b, p, pt: (0, 0))]
        + [pool_spec(n) for n in range(ppb)] * 2,
        out_specs=pl.BlockSpec((T_PAD, W), lambda b, p, pt: (b, 0)),
        scratch_shapes=[pltpu.VMEM((H, LANES, HEAD), BF16), pltpu.VMEM((H, G, HEAD), F32),
                        pltpu.VMEM((1, LANES), F32)],
    )
    return pl.pallas_call(
        functools.partial(_sb_sample_body, n_heads=H, ppb=ppb, t_valid=t_valid),
        grid_spec=grid_spec,
        out_shape=jax.ShapeDtypeStruct((B * T_PAD, W), out_dtype),
        compiler_params=_cparams("parallel", "arbitrary"),
        name="sb_sample",
    )(page_table, proj, proj, proj, bias_b, u_t, *([rows_view(pool_k)] * ppb), *([rows_view(pool_v)] * ppb))


def _rope_tables(pos):
    half = HEAD // 2
    inv = 1.0 / (ROPE_BASE ** jnp.linspace(0.0, 1.0, half, dtype=F32))
    ang = pos[:, None] * inv[None, :]
    cos, sin = jnp.cos(ang), jnp.sin(ang)
    return jnp.concatenate([cos, cos], axis=-1), jnp.concatenate([-sin, sin], axis=-1)


def _retention_tables(n_heads, chunk):
    log_g = np.log(1.0 - 2.0 ** (-5.0 - np.arange(n_heads, dtype=np.float64)))
    idx = np.arange(HEAD, dtype=np.float64)
    diff = idx[:, None] - idx[None, :]
    dmask = np.where(diff >= 0, np.exp(log_g[:, None, None] * np.maximum(diff, 0.0)), 0.0)
    qdec = np.exp(log_g[:, None] * (idx + 1.0)[None, :])
    kdec = np.where(idx[None, :] < chunk, np.exp(log_g[:, None] * (chunk - 1.0 - idx)[None, :]), 0.0)
    bc = lambda t: jnp.asarray(np.broadcast_to(t[:, :, None], (n_heads, HEAD, HEAD)), dtype=F32)
    gch = tuple(float(x) for x in np.exp(log_g * chunk))
    return jnp.asarray(dmask, dtype=F32), bc(qdec), bc(kdec), gch


def kernel(x_prompt, x_sample, mem_prompt, state_ret, cache_sb_k, cache_sb_v, page_table, cache_mem_k, cache_mem_v, norm_mix_g, norm_x_g, norm_ffn_g, w_in_even, w_out_even, sg_norm_g, sg_w_s, sg_b, w_qkv_sb, w_o_sb, sb_bias, w_q_x, w_k_x, w_v_x, w_o_x, q_norm_x_g, k_norm_x_g, w_gu_dense, w_down_dense, w_router, w_gu_moe, w_down_moe):
    B, S, D = x_prompt.shape
    BS, T, _ = x_sample.shape
    depth = norm_mix_g.shape[0]
    n_heads_ret = state_ret.shape[2]
    n_experts = w_router.shape[2]
    past = page_table.shape[1] * PAGE
    H_sb = cache_sb_k.shape[3]
    W_sb = H_sb * HEAD
    W_x = w_q_x.shape[2]

    bf = lambda w: w.astype(BF16)
    TM = min(1024, B * S)
    MS = BS * T_PAD

    xp = x_prompt.reshape(B * S, D)
    xs = jnp.pad(x_sample, ((0, 0), (0, T_PAD - T), (0, 0))).reshape(MS, D)

    cos_p, sin_p = _rope_tables(jnp.arange(S, dtype=F32))
    cos_s, sin_s = _rope_tables(past + jnp.arange(HEAD, dtype=F32))
    dmask, qdec, kdec_p, gch_p = _retention_tables(n_heads_ret, HEAD)
    _, _, kdec_s, gch_s = _retention_tables(n_heads_ret, T)

    mk_p, mv_p = mem_kv(mem_prompt, bf(w_k_x), bf(w_v_x), k_norm_x_g)

    ret_p, ret_s, sgv_s = [], [], []
    sbk_p, sbv_p, sbk_s, sbv_s = [], [], [], []
    for l in range(depth):
        i = l // 2
        sample_f32 = l + 2 < depth
        if l % 2 == 0:
            w_in, w_out = bf(w_in_even[i]), bf(w_out_even[i])
            sgb = jnp.broadcast_to(sg_b[i][:, :, None], sg_b[i].shape + (HEAD,))
            proj = norm_mm(xp, norm_mix_g[l], w_in, 6, TM, 512)
            y, sp = even_core(proj, B, S, HEAD, cos_p, sin_p, dmask, qdec, kdec_p, gch_p,
                              sg_norm_g[i], sg_w_s[i], sgb, None, False, BF16)
            xp = mm_res(y, w_out, xp, TM, 512)
            if sample_f32:
                w_in, w_out = w_in_even[i], w_out_even[i]
            proj = norm_mm(xs, norm_mix_g[l], w_in, 6, MS, 512)
            y, ss, vs = even_core(proj, BS, T_PAD, T_PAD, cos_s, sin_s, dmask, qdec, kdec_s, gch_s,
                                  sg_norm_g[i], sg_w_s[i], sgb, state_ret[i], True, w_in.dtype)
            xs = mm_res(y, w_out, xs, MS, 512)
            ret_p.append(sp)
            ret_s.append(ss)
            sgv_s.append(vs.reshape(BS, T_PAD, -1)[:, :T])
        else:
            w_qkv, w_o = bf(w_qkv_sb[i]), bf(w_o_sb[i])
            proj, qkv16 = norm_mm(xp, norm_mix_g[l], w_qkv, 3, TM, 512, LOG2E * HEAD ** -0.5)
            o = sb_prompt(qkv16, sb_bias[i], B, S, 256)
            xp = mm_res(o, w_o, xp, TM, 512)
            sbk_p.append(proj[1].reshape(B, S, H_sb, HEAD))
            sbv_p.append(proj[2].reshape(B, S, H_sb, HEAD))
            if sample_f32:
                w_qkv, w_o = w_qkv_sb[i], w_o_sb[i]
            proj = norm_mm(xs, norm_mix_g[l], w_qkv, 3, MS, 512)
            o = sb_sample(proj, sb_bias[i], cache_sb_k, cache_sb_v, i, page_table, T, 4, w_o.dtype)
            xs = mm_res(o, w_o, xs, MS, 512)
            sbk_s.append(proj[1].reshape(BS, T_PAD, H_sb, HEAD)[:, :T])
            sbv_s.append(proj[2].reshape(BS, T_PAD, H_sb, HEAD)[:, :T])
        wq, wo = bf(w_q_x[l]), bf(w_o_x[l])
        xp = cross_attn(xp.reshape(B, S, D), norm_x_g[l], wq, q_norm_x_g[l], mk_p, mv_p, l, wo, 512).reshape(B * S, D)
        if sample_f32:
            wq, wo = w_q_x[l], w_o_x[l]
        xs = cross_attn(xs.reshape(BS, T_PAD, D), norm_x_g[l], wq, q_norm_x_g[l], cache_mem_k, cache_mem_v, l, wo, T_PAD).reshape(MS, D)
        if l % 2 == 0:
            w_gu, w_down = bf(w_gu_dense[i]), bf(w_down_dense[i])
            xp = ffn(xp, norm_ffn_g[l], w_gu, w_down, TM, 256)
            if sample_f32:
                w_gu, w_down = w_gu_dense[i], w_down_dense[i]
            xs = ffn(xs, norm_ffn_g[l], w_gu, w_down, MS, 256)
        else:
            w_gu, w_down = bf(w_gu_moe[i]), bf(w_down_moe[i])
            wr = jnp.pad(w_router[i], ((0, 0), (0, LANES - n_experts)))

            def moe(x, tm):
                gates = router(x, norm_ffn_g[l], wr, n_experts, tm)
                gates_t = jnp.transpose(gates[:, :n_experts])[:, :, None]
                return moe_dense(x, norm_ffn_g[l], gates_t, w_gu, w_down, tm)

            xp, xs = moe(xp, min(512, B * S)), moe(xs, MS)

    y_p = xp.reshape(B, S, D)
    y_s = xs.reshape(BS, T_PAD, D)[:, :T]
    return (y_p, y_s, jnp.stack(ret_p), jnp.stack(ret_s), jnp.stack(sgv_s),
            jnp.stack(sbk_p), jnp.stack(sbv_p), jnp.stack(sbk_s), jnp.stack(sbv_s), mk_p, mv_p)
```

```python
import functools
import math

import numpy as np
import jax
import jax.numpy as jnp
from jax import lax
from jax.experimental import pallas as pl
from jax.experimental.pallas import tpu as pltpu

F32 = jnp.float32
BF16 = jnp.bfloat16
EPS = 1e-6
ROPE_BASE = 10000.0
LOG2E = math.log2(math.e)
LANES = 128
VMEM_LIMIT_BYTES = 56 * 1024 * 1024
HEAD = 128
PAGE = 128
T_PAD = 8
NN = (((1,), (0,)), ((), ()))
NT = (((1,), (1,)), ((), ()))
TN = (((0,), (0,)), ((), ()))


def _cparams(*sem):
    return pltpu.CompilerParams(dimension_semantics=sem, vmem_limit_bytes=VMEM_LIMIT_BYTES)


def _rms(x):
    return x * lax.rsqrt(jnp.mean(x * x, axis=-1, keepdims=True) + EPS)


def _gelu(x):
    return 0.5 * x * (1.0 + lax.erf(x * (2.0 ** -0.5)))


def _silu(x):
    return x * jax.nn.sigmoid(x)


def _mm(a, b, dims=NN):
    assert a.dtype == b.dtype
    precision = lax.Precision.HIGHEST if a.dtype == F32 else None
    return lax.dot_general(a, b, dims, precision=precision, preferred_element_type=F32)


def _norm_mm_body(x_ref, g_ref, w_ref, o_ref, h_ref):
    @pl.when(pl.program_id(1) == 0)
    def _():
        h_ref[...] = (_rms(x_ref[...]) * g_ref[...]).astype(h_ref.dtype)

    o_ref[...] = _mm(h_ref[...], w_ref[...])


def norm_mm(x, g, w, groups, tm, tn):
    M, K = x.shape
    N = w.shape[1]
    ng = N // groups
    npg = ng // tn
    return pl.pallas_call(
        _norm_mm_body,
        grid=(M // tm, N // tn),
        in_specs=[
            pl.BlockSpec((tm, K), lambda i, j: (i, 0)),
            pl.BlockSpec((1, K), lambda i, j: (0, 0)),
            pl.BlockSpec((K, tn), lambda i, j: (0, j)),
        ],
        out_specs=pl.BlockSpec((None, tm, tn), lambda i, j: (j // npg, i, j % npg)),
        out_shape=jax.ShapeDtypeStruct((groups, M, ng), F32),
        scratch_shapes=[pltpu.VMEM((tm, K), w.dtype)],
        compiler_params=_cparams("parallel", "arbitrary"),
        name="norm_mm",
    )(x, g.reshape(1, K), w)


def _norm_qkv_body(x_ref, g_ref, w_ref, *rest, q_scale, n_heads, n_alias):
    qkv_ref, k_ref, v_ref, h_ref = rest[n_alias:]
    grp = pl.program_id(1)

    @pl.when(grp == 0)
    def _():
        h_ref[...] = (_rms(x_ref[...]) * g_ref[...]).astype(h_ref.dtype)

    o = _mm(h_ref[...], w_ref[...])
    qkv_ref[...] = (o * jnp.where(grp == 0, q_scale, 1.0)).astype(qkv_ref.dtype)

    def heads_out(ref):
        for h in range(n_heads):
            ref[:, h, :] = o[:, h * HEAD:(h + 1) * HEAD]

    pl.when(grp == 1)(lambda: heads_out(k_ref))
    pl.when(grp == 2)(lambda: heads_out(v_ref))


def norm_qkv(x, g, w, q_scale, layer_slot, n_slots, kv_prev, tm):
    M, K = x.shape
    W = w.shape[1] // 3
    nh = W // HEAD
    kv_shape = jax.ShapeDtypeStruct((n_slots, M, nh, HEAD), F32)
    kv_spec = pl.BlockSpec((None, tm, nh, HEAD), lambda i, j: (layer_slot, i, 0, 0))
    n_alias = 0 if kv_prev is None else 2
    return pl.pallas_call(
        functools.partial(_norm_qkv_body, q_scale=q_scale, n_heads=nh, n_alias=n_alias),
        grid=(M // tm, 3),
        in_specs=[
            pl.BlockSpec((tm, K), lambda i, j: (i, 0)),
            pl.BlockSpec((1, K), lambda i, j: (0, 0)),
            pl.BlockSpec((K, W), lambda i, j: (0, j)),
        ] + [pl.BlockSpec(memory_space=pl.ANY)] * n_alias,
        out_specs=[pl.BlockSpec((None, tm, W), lambda i, j: (j, i, 0)), kv_spec, kv_spec],
        out_shape=[jax.ShapeDtypeStruct((3, M, W), BF16), kv_shape, kv_shape],
        input_output_aliases={3: 1, 4: 2} if n_alias else {},
        scratch_shapes=[pltpu.VMEM((tm, K), w.dtype)],
        compiler_params=_cparams("parallel", "arbitrary"),
        name="norm_qkv",
    )(x, g.reshape(1, K), w, *(kv_prev or ()))


def _mm_res_body(a_ref, w_ref, r_ref, o_ref):
    o_ref[...] = r_ref[...] + _mm(a_ref[...], w_ref[...])


def mm_res(a, w, res, tm, tn):
    M, K = a.shape
    N = w.shape[1]
    tn = min(tn, N)
    return pl.pallas_call(
        _mm_res_body,
        grid=(M // tm, N // tn),
        in_specs=[
            pl.BlockSpec((tm, K), lambda i, j: (i, 0)),
            pl.BlockSpec((K, tn), lambda i, j: (0, j)),
            pl.BlockSpec((tm, tn), lambda i, j: (i, j)),
        ],
        out_specs=pl.BlockSpec((tm, tn), lambda i, j: (i, j)),
        out_shape=jax.ShapeDtypeStruct((M, N), F32),
        compiler_params=_cparams("parallel", "arbitrary"),
        name="mm_res",
    )(a, w, res)


def _even_core_body(*refs, n_heads, n_groups, gch, rows_in, has_s0, emit_vb):
    it = iter(refs)
    p_ref, cos_ref, sin_ref, dmask_ref, qdec_ref, kdec_ref, sgg_ref, ws_ref, sgb_ref = (next(it) for _ in range(9))
    s0_ref = next(it) if has_s0 else None
    y_ref, sout_ref = next(it), next(it)
    vb_ref = next(it) if emit_vb else None
    s_ref = next(it)
    C = HEAD
    cd = y_ref.dtype
    c = pl.program_id(1)
    w_ret = n_heads * HEAD

    @pl.when(c == 0)
    def _():
        if has_s0:
            s_ref[...] = s0_ref[...]
        else:
            s_ref[...] = jnp.zeros_like(s_ref)

    def rows(x):
        if rows_in == C:
            return x
        return jnp.concatenate([x, jnp.zeros((C - rows_in, x.shape[1]), x.dtype)], axis=0)

    cos = cos_ref[...]
    sin = sin_ref[...]

    def rot(x):
        return x * cos + pltpu.roll(x, HEAD // 2, 1) * sin

    for h in range(n_heads):
        sl = slice(h * HEAD, (h + 1) * HEAD)
        q = rot(rows(p_ref[0, :, sl]))
        k = rot(rows(p_ref[1, :, sl])) * (HEAD ** -0.5)
        v = rows(p_ref[2, :, sl]).astype(cd)
        g = rows(p_ref[3, :, sl])
        intra = _mm(q.astype(cd), k.astype(cd), NT) * dmask_ref[h]
        o = _mm(intra.astype(cd), v)
        s = s_ref[h]
        o = o + _mm((q * qdec_ref[h]).astype(cd), s.astype(cd))
        kv = _mm((k * kdec_ref[h]).astype(cd), v, TN)
        s_ref[h] = gch[h] * s + kv
        oa = _rms(o) * _silu(g)
        y_ref[:, sl] = oa[:rows_in].astype(y_ref.dtype)

    u = _gelu(rows(p_ref[4]))
    vg = _gelu(rows(p_ref[5]))
    xc = vg - jnp.mean(vg, axis=-1, keepdims=True)
    vb = xc * lax.rsqrt(jnp.mean(xc * xc, axis=-1, keepdims=True) + EPS) * sgg_ref[...]
    if emit_vb:
        vb_ref[...] = vb[:rows_in]
    row = lax.broadcasted_iota(jnp.int32, (C, C), 0)
    col = lax.broadcasted_iota(jnp.int32, (C, C), 1)
    for gi in range(n_groups):
        sl = slice(gi * HEAD, (gi + 1) * HEAD)
        w = jnp.where(col <= row, ws_ref[gi], 0.0).astype(cd)
        f = _mm(w, vb[:, sl].astype(cd)) + sgb_ref[gi]
        ob = u[:, sl] * f
        y_ref[:, w_ret + gi * HEAD:w_ret + (gi + 1) * HEAD] = ob[:rows_in].astype(y_ref.dtype)

    @pl.when(c == pl.num_programs(1) - 1)
    def _():
        sout_ref[...] = s_ref[...]


def even_core(proj, B, L, rows_in, cos, sin, dmask, qdec, kdec, gch, sg_g, w_s, sgb, s0, emit_vb, out_dtype):
    n_heads = dmask.shape[0]
    n_groups = w_s.shape[0]
    W = proj.shape[2]
    nc = L // rows_in
    has_s0 = s0 is not None
    const3 = lambda b, c: (0, 0, 0)
    in_specs = [
        pl.BlockSpec((6, rows_in, W), lambda b, c: (0, b * nc + c, 0)),
        pl.BlockSpec((HEAD, HEAD), lambda b, c: (c, 0)),
        pl.BlockSpec((HEAD, HEAD), lambda b, c: (c, 0)),
        pl.BlockSpec(dmask.shape, const3),
        pl.BlockSpec(qdec.shape, const3),
        pl.BlockSpec(kdec.shape, const3),
        pl.BlockSpec((1, W), lambda b, c: (0, 0)),
        pl.BlockSpec(w_s.shape, const3),
        pl.BlockSpec(sgb.shape, const3),
    ]
    args = [proj, cos, sin, dmask, qdec, kdec, sg_g.reshape(1, W), w_s, sgb]
    if has_s0:
        in_specs.append(pl.BlockSpec((None, n_heads, HEAD, HEAD), lambda b, c: (b, 0, 0, 0)))
        args.append(s0)
    out_specs = [
        pl.BlockSpec((rows_in, 2 * W), lambda b, c: (b * nc + c, 0)),
        pl.BlockSpec((None, n_heads, HEAD, HEAD), lambda b, c: (b, 0, 0, 0)),
    ]
    out_shape = [
        jax.ShapeDtypeStruct((B * L, 2 * W), out_dtype),
        jax.ShapeDtypeStruct((B, n_heads, HEAD, HEAD), F32),
    ]
    if emit_vb:
        out_specs.append(pl.BlockSpec((rows_in, W), lambda b, c: (b * nc + c, 0)))
        out_shape.append(jax.ShapeDtypeStruct((B * L, W), F32))
    body = functools.partial(_even_core_body, n_heads=n_heads, n_groups=n_groups, gch=gch,
                             rows_in=rows_in, has_s0=has_s0, emit_vb=emit_vb)
    return pl.pallas_call(
        body,
        grid=(B, nc),
        in_specs=in_specs,
        out_specs=out_specs,
        out_shape=out_shape,
        scratch_shapes=[pltpu.VMEM((n_heads, HEAD, HEAD), F32)],
        compiler_params=_cparams("parallel", "arbitrary"),
        name="even_core",
    )(*args)


def _mem_kv_body(m_ref, wk_ref, wv_ref, kg_ref, k_ref, v_ref, k16_ref, v16_ref, *, n_heads):
    m = m_ref[...].astype(BF16)
    k = jnp.dot(m, wk_ref[...], preferred_element_type=F32)
    kg = kg_ref[...]
    v = jnp.dot(m, wv_ref[...], preferred_element_type=F32)
    for h in range(n_heads):
        sl = slice(h * HEAD, (h + 1) * HEAD)
        kn = _rms(k[:, sl]) * kg
        k_ref[:, h, :] = kn
        v_ref[:, h, :] = v[:, sl]
        k16_ref[h] = kn.astype(BF16)
        v16_ref[h] = v[:, sl].astype(BF16)


def mem_kv(mem, w_k, w_v, k_g):
    B, NM, D = mem.shape
    depth, _, WX = w_k.shape
    nh = WX // HEAD
    out = jax.ShapeDtypeStruct((depth, B, NM, nh, HEAD), F32)
    out16 = jax.ShapeDtypeStruct((depth, B, nh, NM, HEAD), BF16)
    return pl.pallas_call(
        functools.partial(_mem_kv_body, n_heads=WX // HEAD),
        grid=(depth, B),
        in_specs=[
            pl.BlockSpec((None, NM, D), lambda l, b: (b, 0, 0)),
            pl.BlockSpec((None, D, WX), lambda l, b: (l, 0, 0)),
            pl.BlockSpec((None, D, WX), lambda l, b: (l, 0, 0)),
            pl.BlockSpec((None, 1, HEAD), lambda l, b: (l, 0, 0)),
        ],
        out_specs=[pl.BlockSpec((None, None, NM, nh, HEAD), lambda l, b: (l, b, 0, 0, 0))] * 2
        + [pl.BlockSpec((None, None, nh, NM, HEAD), lambda l, b: (l, b, 0, 0, 0))] * 2,
        out_shape=[out, out, out16, out16],
        compiler_params=_cparams("parallel", "parallel"),
        name="mem_kv",
    )(mem, w_k, w_v, k_g.reshape(depth, 1, HEAD))


def _mem_attention(q, qg, mk_ref, mv_ref, cd):
    outs = []
    for h in range(mk_ref.shape[0]):
        qn = (_rms(q[:, h * HEAD:(h + 1) * HEAD]) * qg).astype(cd)
        s = _mm(qn, mk_ref[h], NT)
        e = jnp.exp2(s - jnp.max(s, axis=-1, keepdims=True))
        outs.append(_mm(e.astype(cd), mv_ref[h]) * (1.0 / jnp.sum(e, axis=-1, keepdims=True)))
    return jnp.concatenate(outs, axis=-1).astype(cd)


def _cross_attn_body(x_ref, g_ref, wq_ref, qg_ref, mk_ref, mv_ref, wo_ref, o_ref):
    cd = wq_ref.dtype
    x = x_ref[...]
    q = _mm((_rms(x) * g_ref[...]).astype(cd), wq_ref[...])
    o_ref[...] = x + _mm(_mem_attention(q, qg_ref[...], mk_ref, mv_ref, cd), wo_ref[...])


def _mem_attn_body(q_ref, qg_ref, mk_ref, mv_ref, o_ref):
    o_ref[...] = _mem_attention(q_ref[...], qg_ref[...], mk_ref, mv_ref, o_ref.dtype)


def mem_attn(q, qg, mk_all, mv_all, layer):
    B, L, WX = q.shape
    nh, NM = mk_all.shape[2], mk_all.shape[3]
    mem_spec = pl.BlockSpec((None, None, nh, NM, HEAD), lambda b: (layer, b, 0, 0, 0))
    return pl.pallas_call(
        _mem_attn_body,
        grid=(B,),
        in_specs=[
            pl.BlockSpec((None, L, WX), lambda b: (b, 0, 0)),
            pl.BlockSpec((1, HEAD), lambda b: (0, 0)),
            mem_spec,
            mem_spec,
        ],
        out_specs=pl.BlockSpec((None, L, WX), lambda b: (b, 0, 0)),
        out_shape=jax.ShapeDtypeStruct((B, L, WX), mk_all.dtype),
        compiler_params=_cparams("parallel"),
        name="mem_attn",
    )(q, (qg * (LOG2E * HEAD ** -0.5)).reshape(1, HEAD), mk_all, mv_all)


def cross_attn(x, g, wq, qg, mk_all, mv_all, layer, wo, tm):
    B, L, D = x.shape
    nh, NM = mk_all.shape[2], mk_all.shape[3]
    WX = nh * HEAD
    qg = qg * (LOG2E * HEAD ** -0.5)
    mem_spec = pl.BlockSpec((None, None, nh, NM, HEAD), lambda b, i: (layer, b, 0, 0, 0))
    return pl.pallas_call(
        _cross_attn_body,
        grid=(B, L // tm),
        in_specs=[
            pl.BlockSpec((None, tm, D), lambda b, i: (b, i, 0)),
            pl.BlockSpec((1, D), lambda b, i: (0, 0)),
            pl.BlockSpec((D, WX), lambda b, i: (0, 0)),
            pl.BlockSpec((1, HEAD), lambda b, i: (0, 0)),
            mem_spec,
            mem_spec,
            pl.BlockSpec((WX, D), lambda b, i: (0, 0)),
        ],
        out_specs=pl.BlockSpec((None, tm, D), lambda b, i: (b, i, 0)),
        out_shape=jax.ShapeDtypeStruct((B, L, D), F32),
        compiler_params=_cparams("parallel", "parallel"),
        name="cross_attn",
    )(x, g.reshape(1, D), wq, qg.reshape(1, HEAD), mk_all, mv_all, wo)


def _ffn_body(x_ref, g_ref, wg_ref, wu_ref, wd_ref, o_ref, h_ref, acc_ref):
    f = pl.program_id(1)

    @pl.when(f == 0)
    def _():
        h_ref[...] = (_rms(x_ref[...]) * g_ref[...]).astype(h_ref.dtype)
        acc_ref[...] = jnp.zeros_like(acc_ref)

    h = h_ref[...]
    a = _silu(_mm(h, wg_ref[...])) * _mm(h, wu_ref[...])
    acc_ref[...] += _mm(a.astype(h.dtype), wd_ref[...])

    @pl.when(f == pl.num_programs(1) - 1)
    def _():
        o_ref[...] = x_ref[...] + acc_ref[...]


def ffn(x, g, w_gu, w_down, tm, tf):
    M, D = x.shape
    F = w_down.shape[0]
    nf = F // tf
    return pl.pallas_call(
        _ffn_body,
        grid=(M // tm, nf),
        in_specs=[
            pl.BlockSpec((tm, D), lambda i, f: (i, 0)),
            pl.BlockSpec((1, D), lambda i, f: (0, 0)),
            pl.BlockSpec((D, tf), lambda i, f: (0, f)),
            pl.BlockSpec((D, tf), lambda i, f: (0, nf + f)),
            pl.BlockSpec((tf, D), lambda i, f: (f, 0)),
        ],
        out_specs=pl.BlockSpec((tm, D), lambda i, f: (i, 0)),
        out_shape=jax.ShapeDtypeStruct((M, D), F32),
        scratch_shapes=[pltpu.VMEM((tm, D), w_gu.dtype), pltpu.VMEM((tm, D), F32)],
        compiler_params=_cparams("parallel", "arbitrary"),
        name="ffn",
    )(x, g.reshape(1, D), w_gu, w_gu, w_down)


def _router_body(x_ref, g_ref, wr_ref, o_ref, *, n_experts):
    h = _rms(x_ref[...]) * g_ref[...]
    logits = jnp.dot(h, wr_ref[...], preferred_element_type=F32, precision=lax.Precision.HIGHEST)
    lane = lax.broadcasted_iota(jnp.int32, logits.shape, 1).astype(F32)
    neg = jnp.float32(-jnp.inf)
    logits = jnp.where(lane < n_experts, logits, neg)
    m1 = jnp.max(logits, axis=-1, keepdims=True)
    i1 = jnp.min(jnp.where(logits == m1, lane, float(LANES)), axis=-1, keepdims=True)
    rest = jnp.where(lane == i1, neg, logits)
    m2 = jnp.max(rest, axis=-1, keepdims=True)
    i2 = jnp.min(jnp.where(rest == m2, lane, float(LANES)), axis=-1, keepdims=True)
    e2 = jnp.exp(m2 - m1)
    den = 1.0 + e2
    o_ref[...] = jnp.where(lane == i1, 1.0 / den, 0.0) + jnp.where(lane == i2, e2 / den, 0.0)


def router(x, g, w_router_pad, n_experts, tm):
    M, D = x.shape
    return pl.pallas_call(
        functools.partial(_router_body, n_experts=n_experts),
        grid=(M // tm,),
        in_specs=[
            pl.BlockSpec((tm, D), lambda i: (i, 0)),
            pl.BlockSpec((1, D), lambda i: (0, 0)),
            pl.BlockSpec((D, LANES), lambda i: (0, 0)),
        ],
        out_specs=pl.BlockSpec((tm, LANES), lambda i: (i, 0)),
        out_shape=jax.ShapeDtypeStruct((M, LANES), F32),
        compiler_params=_cparams("parallel"),
        name="router",
    )(x, g.reshape(1, D), w_router_pad)


def _moe_body(x_ref, g_ref, gate_ref, wg_ref, wu_ref, wd_ref, o_ref, h_ref, acc_ref):
    e = pl.program_id(1)

    @pl.when(e == 0)
    def _():
        h_ref[...] = (_rms(x_ref[...]) * g_ref[...]).astype(BF16)
        acc_ref[...] = jnp.zeros_like(acc_ref)

    h = h_ref[...]
    a = _silu(jnp.dot(h, wg_ref[...], preferred_element_type=F32)) * jnp.dot(h, wu_ref[...], preferred_element_type=F32)
    acc_ref[...] += gate_ref[...] * jnp.dot(a.astype(BF16), wd_ref[...], preferred_element_type=F32)

    @pl.when(e == pl.num_programs(1) - 1)
    def _():
        o_ref[...] = x_ref[...] + acc_ref[...]


def moe_dense(x, g, gates_t, w_gu, w_down, tm):
    M, D = x.shape
    E, F, _ = w_down.shape
    return pl.pallas_call(
        _moe_body,
        grid=(M // tm, E),
        in_specs=[
            pl.BlockSpec((tm, D), lambda i, e: (i, 0)),
            pl.BlockSpec((1, D), lambda i, e: (0, 0)),
            pl.BlockSpec((None, tm, 1), lambda i, e: (e, i, 0)),
            pl.BlockSpec((None, D, F), lambda i, e: (e, 0, 0)),
            pl.BlockSpec((None, D, F), lambda i, e: (e, 0, 1)),
            pl.BlockSpec((None, F, D), lambda i, e: (e, 0, 0)),
        ],
        out_specs=pl.BlockSpec((tm, D), lambda i, e: (i, 0)),
        out_shape=jax.ShapeDtypeStruct((M, D), F32),
        scratch_shapes=[pltpu.VMEM((tm, D), BF16), pltpu.VMEM((tm, D), F32)],
        compiler_params=_cparams("parallel", "arbitrary"),
        name="moe",
    )(x, g.reshape(1, D), gates_t, w_gu, w_gu, w_down)


def _log2_terms(z2):
    sign_bit = jnp.int32(-2 ** 31)
    neg_abs = lax.bitcast_convert_type(lax.bitcast_convert_type(z2, jnp.int32) | sign_bit, F32)
    nlom = jnp.maximum(z2, 0.0) + jnp.log2(1.0 + jnp.exp2(neg_abs))
    return nlom, z2 - nlom


MASKED_LOG2 = -1e30


def _sb_prompt_body(q_ref, k_ref, v_ref, b_ref, u_ref, o_ref, acc_ref, r_ref, lom_ref, lb_ref, *, tk, rq):
    i = pl.program_id(2)
    tq = rq * tk
    n = rq * (i + 1)
    bias = b_ref[...]
    acc_ref[...] = jnp.zeros_like(acc_ref)
    r_ref[...] = jnp.zeros_like(r_ref)

    def keys(ref, kb):
        return ref[pl.ds(pl.multiple_of(kb * tk, tk), tk), :]

    def stage_a(kb, slot, masked):
        z = lax.dot_general(q_ref[...], keys(k_ref, kb), NT, preferred_element_type=F32) + bias
        nlom, lbeta = _log2_terms(z)
        if masked:
            row = lax.broadcasted_iota(jnp.int32, (tq, tk), 0)
            col = lax.broadcasted_iota(jnp.int32, (tq, tk), 1)
            keep = col + kb * tk < row + i * tq
            nlom = jnp.where(keep, nlom, 0.0)
            lbeta = jnp.where(keep, lbeta, MASKED_LOG2)
        lom_ref[slot] = nlom.astype(BF16)
        lb_ref[slot] = lbeta

    def stage_b(kb, slot):
        c = jnp.dot(lom_ref[slot], u_ref[...], preferred_element_type=F32)
        r = r_ref[...]
        a = jnp.exp2(lb_ref[slot] - c[:, :tk] - jnp.concatenate([r] * (tk // LANES), axis=1))
        acc_ref[...] += jnp.dot(a.astype(BF16), keys(v_ref, kb), preferred_element_type=F32)
        r_ref[...] = r + c[:, tk:]

    stage_a(n - 1, 0, True)
    for t in range(1, rq):
        stage_a(n - 1 - t, t % 2, True)
        stage_b(n - t, (t - 1) % 2)

    def body(p, carry):
        kb = n - 1 - rq - 2 * p
        stage_a(kb, 0, False)
        stage_b(kb + 1, 1)
        stage_a(kb - 1, 1, False)
        stage_b(kb, 0)
        return carry

    lax.fori_loop(0, (n - rq) // 2, body, 0)
    stage_b(0, 1)
    o_ref[...] = acc_ref[...].astype(o_ref.dtype)


def _suffix_matrix(t):
    j = np.arange(t)[:, None]
    s = np.arange(t)[None, :]
    u = np.concatenate([(j > s).astype(np.float32), np.ones((t, LANES), np.float32)], axis=1)
    return jnp.asarray(u, dtype=BF16)


def sb_prompt(qkv, bias, B, S, tk, rq):
    W = qkv.shape[2]
    H = W // HEAD
    assert rq % 2 == 0
    tq = rq * tk
    nq = S // tq
    bias_b = jnp.broadcast_to((bias.astype(F32) * LOG2E)[:, None, None], (H, 1, tk))
    return pl.pallas_call(
        functools.partial(_sb_prompt_body, tk=tk, rq=rq),
        grid=(B, H, nq),
        in_specs=[
            pl.BlockSpec((None, tq, HEAD), lambda b, h, i: (0, b * nq + i, h)),
            pl.BlockSpec((None, S, HEAD), lambda b, h, i: (1, b, h)),
            pl.BlockSpec((None, S, HEAD), lambda b, h, i: (2, b, h)),
            pl.BlockSpec((None, 1, tk), lambda b, h, i: (h, 0, 0)),
            pl.BlockSpec((tk, tk + LANES), lambda b, h, i: (0, 0)),
        ],
        out_specs=pl.BlockSpec((tq, HEAD), lambda b, h, i: (b * nq + i, h)),
        out_shape=jax.ShapeDtypeStruct((B * S, W), BF16),
        scratch_shapes=[pltpu.VMEM((tq, HEAD), F32), pltpu.VMEM((tq, LANES), F32),
                        pltpu.VMEM((2, tq, tk), BF16), pltpu.VMEM((2, tq, tk), F32)],
        compiler_params=_cparams("parallel", "parallel", "arbitrary"),
        name="sb_prompt",
    )(qkv, qkv, qkv, bias_b, _suffix_matrix(tk))


def _sb_sample_body(pt_ref, q_ref, kn_ref, vn_ref, b_ref, u_ref, *rest, n_heads, ppb, t_valid):
    kp_refs = rest[:ppb]
    vp_refs = rest[ppb:2 * ppb]
    o_ref, qx_ref, acc_ref, r_ref = rest[2 * ppb:]
    del pt_ref
    pg = pl.program_id(1)
    G = LANES // n_heads
    bias = b_ref[...]
    col = lax.broadcasted_iota(jnp.int32, (PAGE, LANES), 1)
    row = lax.broadcasted_iota(jnp.int32, (PAGE, LANES), 0)

    def pages(get_k, get_v, npages, masked):
        z = bias
        for h in range(n_heads):
            k_h = jnp.concatenate([get_k(n, h) for n in range(npages)], axis=0)
            z = z + lax.dot_general(k_h, qx_ref[h], NT, preferred_element_type=F32)
        nlom, lbeta = _log2_terms(z)
        if masked:
            t = col & (G - 1)
            keep = (row < t) & (t < t_valid)
            nlom = jnp.where(keep, nlom, 0.0)
        nlom_p = [nlom[n * PAGE:(n + 1) * PAGE] for n in range(npages)]
        after = jnp.dot(u_ref[...], jnp.concatenate(nlom_p, axis=1).astype(BF16), preferred_element_type=F32)
        r = r_ref[...]
        a_t = []
        for n in range(npages):
            a = jnp.exp2(lbeta[n * PAGE:(n + 1) * PAGE] - after[:, n * LANES:(n + 1) * LANES] - r)
            if masked:
                a = jnp.where(keep, a, 0.0)
            a_t.append(a.T.astype(BF16))
            r = r + jnp.sum(nlom_p[n], axis=0, keepdims=True)
        r_ref[...] = r
        a_t = jnp.concatenate(a_t, axis=1)
        for h in range(n_heads):
            v_h = jnp.concatenate([get_v(n, h) for n in range(npages)], axis=0)
            acc_ref[h] += jnp.dot(a_t[h * G:(h + 1) * G], v_h, preferred_element_type=F32)

    def pad_rows(x, before, total):
        parts = []
        if before:
            parts.append(jnp.zeros((before, x.shape[1]), x.dtype))
        parts.append(x)
        if total - before - x.shape[0]:
            parts.append(jnp.zeros((total - before - x.shape[0], x.shape[1]), x.dtype))
        return jnp.concatenate(parts, axis=0)

    @pl.when(pg == 0)
    def _():
        for h in range(n_heads):
            sl = slice(h * HEAD, (h + 1) * HEAD)
            qx_ref[h] = pad_rows(q_ref[:, sl] * (LOG2E * HEAD ** -0.5), h * G, LANES).astype(BF16)
        acc_ref[...] = jnp.zeros_like(acc_ref)
        r_ref[...] = jnp.zeros_like(r_ref)

        def new_rows(ref):
            return lambda n, h: pad_rows(ref[:, h * HEAD:(h + 1) * HEAD], 0, PAGE).astype(BF16)

        pages(new_rows(kn_ref), new_rows(vn_ref), 1, True)

    def pool_rows(refs):
        return lambda n, h: refs[n][pl.ds(h, PAGE, stride=n_heads), :].astype(BF16)

    pages(pool_rows(kp_refs), pool_rows(vp_refs), ppb, False)

    @pl.when(pg == pl.num_programs(1) - 1)
    def _():
        for h in range(n_heads):
            o_ref[:, h * HEAD:(h + 1) * HEAD] = acc_ref[h][:T_PAD].astype(o_ref.dtype)


def sb_sample(proj, bias, pool_k, pool_v, layer, page_table, t_valid, ppb, out_dtype):
    W = proj.shape[2]
    H = W // HEAD
    G = LANES // H
    B, n_pages = page_table.shape
    npg = n_pages // ppb
    bias_b = jnp.repeat(bias.astype(F32) * LOG2E, G).reshape(1, LANES)
    j = np.arange(PAGE)
    u_t = jnp.asarray((j[None, :] > j[:, None]).astype(np.float32), dtype=BF16)

    def rows_view(pool):
        return pool.reshape(pool.shape[0], pool.shape[1], PAGE * H, HEAD)

    def tok(g):
        return pl.BlockSpec((None, T_PAD, W), lambda b, p, pt: (g, b, 0))

    def pool_spec(n):
        return pl.BlockSpec((None, None, PAGE * H, HEAD),
                            lambda b, p, pt: (layer, pt[b, n_pages - 1 - (p * ppb + n)], 0, 0))

    grid_spec = pltpu.PrefetchScalarGridSpec(
        num_scalar_prefetch=1,
        grid=(B, npg),
        in_specs=[tok(0), tok(1), tok(2),
                  pl.BlockSpec((1, LANES), lambda b, p, pt: (0, 0)),
                  pl.BlockSpec((PAGE, PAGE), lambda b, p, pt: (0, 0))]
        + [pool_spec(n) for n in range(ppb)] * 2,
        out_specs=pl.BlockSpec((T_PAD, W), lambda b, p, pt: (b, 0)),
        scratch_shapes=[pltpu.VMEM((H, LANES, HEAD), BF16), pltpu.VMEM((H, G, HEAD), F32),
                        pltpu.VMEM((1, LANES), F32)],
    )
    return pl.pallas_call(
        functools.partial(_sb_sample_body, n_heads=H, ppb=ppb, t_valid=t_valid),
        grid_spec=grid_spec,
        out_shape=jax.ShapeDtypeStruct((B * T_PAD, W), out_dtype),
        compiler_params=_cparams("parallel", "arbitrary"),
        name="sb_sample",
    )(page_table, proj, proj, proj, bias_b, u_t, *([rows_view(pool_k)] * ppb), *([rows_view(pool_v)] * ppb))


def _rope_tables(pos):
    half = HEAD // 2
    inv = 1.0 / (ROPE_BASE ** jnp.linspace(0.0, 1.0, half, dtype=F32))
    ang = pos[:, None] * inv[None, :]
    cos, sin = jnp.cos(ang), jnp.sin(ang)
    return jnp.concatenate([cos, cos], axis=-1), jnp.concatenate([-sin, sin], axis=-1)


def _retention_tables(n_heads, chunk):
    log_g = np.log(1.0 - 2.0 ** (-5.0 - np.arange(n_heads, dtype=np.float64)))
    idx = np.arange(HEAD, dtype=np.float64)
    diff = idx[:, None] - idx[None, :]
    dmask = np.where(diff >= 0, np.exp(log_g[:, None, None] * np.maximum(diff, 0.0)), 0.0)
    qdec = np.exp(log_g[:, None] * (idx + 1.0)[None, :])
    kdec = np.where(idx[None, :] < chunk, np.exp(log_g[:, None] * (chunk - 1.0 - idx)[None, :]), 0.0)
    bc = lambda t: jnp.asarray(np.broadcast_to(t[:, :, None], (n_heads, HEAD, HEAD)), dtype=F32)
    gch = tuple(float(x) for x in np.exp(log_g * chunk))
    return jnp.asarray(dmask, dtype=F32), bc(qdec), bc(kdec), gch


def kernel(x_prompt, x_sample, mem_prompt, state_ret, cache_sb_k, cache_sb_v, page_table, cache_mem_k, cache_mem_v, norm_mix_g, norm_x_g, norm_ffn_g, w_in_even, w_out_even, sg_norm_g, sg_w_s, sg_b, w_qkv_sb, w_o_sb, sb_bias, w_q_x, w_k_x, w_v_x, w_o_x, q_norm_x_g, k_norm_x_g, w_gu_dense, w_down_dense, w_router, w_gu_moe, w_down_moe):
    B, S, D = x_prompt.shape
    BS, T, _ = x_sample.shape
    depth = norm_mix_g.shape[0]
    n_heads_ret = state_ret.shape[2]
    n_experts = w_router.shape[2]
    past = page_table.shape[1] * PAGE
    H_sb = cache_sb_k.shape[3]
    W_sb = H_sb * HEAD
    W_x = w_q_x.shape[2]

    bf = lambda w: w.astype(BF16)
    TM = min(1024, B * S)
    MS = BS * T_PAD

    xp = x_prompt.reshape(B * S, D)
    xs = jnp.pad(x_sample, ((0, 0), (0, T_PAD - T), (0, 0))).reshape(MS, D)

    cos_p, sin_p = _rope_tables(jnp.arange(S, dtype=F32))
    cos_s, sin_s = _rope_tables(past + jnp.arange(HEAD, dtype=F32))
    dmask, qdec, kdec_p, gch_p = _retention_tables(n_heads_ret, HEAD)
    _, _, kdec_s, gch_s = _retention_tables(n_heads_ret, T)

    mk_p, mv_p, mk_p16, mv_p16 = mem_kv(mem_prompt, bf(w_k_x), bf(w_v_x), k_norm_x_g)
    cmk = jnp.transpose(cache_mem_k, (0, 1, 3, 2, 4))
    cmv = jnp.transpose(cache_mem_v, (0, 1, 3, 2, 4))
    cmk16, cmv16 = bf(cmk), bf(cmv)

    ret_p, ret_s, sgv_s = [], [], []
    kv_p = None
    sbk_s, sbv_s = [], []
    for l in range(depth):
        i = l // 2
        sample_f32 = l + 2 < depth
        if l % 2 == 0:
            w_in, w_out = bf(w_in_even[i]), bf(w_out_even[i])
            sgb = jnp.broadcast_to(sg_b[i][:, :, None], sg_b[i].shape + (HEAD,))
            proj = norm_mm(xp, norm_mix_g[l], w_in, 6, TM, 512)
            y, sp = even_core(proj, B, S, HEAD, cos_p, sin_p, dmask, qdec, kdec_p, gch_p,
                              sg_norm_g[i], sg_w_s[i], sgb, None, False, BF16)
            xp = mm_res(y, w_out, xp, TM, 512)
            if sample_f32:
                w_in, w_out = w_in_even[i], w_out_even[i]
            proj = norm_mm(xs, norm_mix_g[l], w_in, 6, MS, 512)
            y, ss, vs = even_core(proj, BS, T_PAD, T_PAD, cos_s, sin_s, dmask, qdec, kdec_s, gch_s,
                                  sg_norm_g[i], sg_w_s[i], sgb, state_ret[i], True, w_in.dtype)
            xs = mm_res(y, w_out, xs, MS, 512)
            ret_p.append(sp)
            ret_s.append(ss)
            sgv_s.append(vs.reshape(BS, T_PAD, -1)[:, :T])
        else:
            w_qkv, w_o = bf(w_qkv_sb[i]), bf(w_o_sb[i])
            qkv16, *kv_p = norm_qkv(xp, norm_mix_g[l], w_qkv, LOG2E * HEAD ** -0.5, i, depth // 2, kv_p,
                                    min(512, B * S))
            o = sb_prompt(qkv16, sb_bias[i], B, S, 256, 2)
            xp = mm_res(o, w_o, xp, TM, 512)
            if sample_f32:
                w_qkv, w_o = w_qkv_sb[i], w_o_sb[i]
            proj = norm_mm(xs, norm_mix_g[l], w_qkv, 3, MS, 512)
            o = sb_sample(proj, sb_bias[i], cache_sb_k, cache_sb_v, i, page_table, T, 4, w_o.dtype)
            xs = mm_res(o, w_o, xs, MS, 512)
            sbk_s.append(proj[1].reshape(BS, T_PAD, H_sb, HEAD)[:, :T])
            sbv_s.append(proj[2].reshape(BS, T_PAD, H_sb, HEAD)[:, :T])
        wq, wo = bf(w_q_x[l]), bf(w_o_x[l])
        xp = cross_attn(xp.reshape(B, S, D), norm_x_g[l], wq, q_norm_x_g[l], mk_p16, mv_p16, l, wo, 512).reshape(B * S, D)
        mem_s = (cmk16, cmv16)
        if sample_f32:
            wq, wo, mem_s = w_q_x[l], w_o_x[l], (cmk, cmv)
        q_s = norm_mm(xs, norm_x_g[l], wq, 1, MS, 512)
        o_s = mem_attn(q_s.reshape(BS, T_PAD, -1), q_norm_x_g[l], *mem_s, l)
        xs = mm_res(o_s.reshape(MS, -1), wo, xs, MS, 512)
        if l % 2 == 0:
            w_gu, w_down = bf(w_gu_dense[i]), bf(w_down_dense[i])
            xp = ffn(xp, norm_ffn_g[l], w_gu, w_down, TM, 256)
            if sample_f32:
                w_gu, w_down = w_gu_dense[i], w_down_dense[i]
            xs = ffn(xs, norm_ffn_g[l], w_gu, w_down, MS, 256)
        else:
            w_gu, w_down = bf(w_gu_moe[i]), bf(w_down_moe[i])
            wr = jnp.pad(w_router[i], ((0, 0), (0, LANES - n_experts)))

            def moe(x, tm):
                gates = router(x, norm_ffn_g[l], wr, n_experts, tm)
                gates_t = jnp.transpose(gates[:, :n_experts])[:, :, None]
                return moe_dense(x, norm_ffn_g[l], gates_t, w_gu, w_down, tm)

            xp, xs = moe(xp, min(512, B * S)), moe(xs, MS)

    y_p = xp.reshape(B, S, D)
    y_s = xs.reshape(BS, T_PAD, D)[:, :T]
    return (y_p, y_s, jnp.stack(ret_p), jnp.stack(ret_s), jnp.stack(sgv_s),
            kv_p[0].reshape(-1, B, S, H_sb, HEAD), kv_p[1].reshape(-1, B, S, H_sb, HEAD),
            jnp.stack(sbk_s), jnp.stack(sbv_s), mk_p, mv_p)
```

```python
import functools
import math

import numpy as np
import jax
import jax.numpy as jnp
from jax import lax
from jax.experimental import pallas as pl
from jax.experimental.pallas import tpu as pltpu

F32 = jnp.float32
BF16 = jnp.bfloat16
EPS = 1e-6
ROPE_BASE = 10000.0
LOG2E = math.log2(math.e)
LANES = 128
VMEM_LIMIT_BYTES = 56 * 1024 * 1024
HEAD = 128
PAGE = 128
T_PAD = 8
NN = (((1,), (0,)), ((), ()))
NT = (((1,), (1,)), ((), ()))
TN = (((0,), (0,)), ((), ()))


def _cparams(*sem):
    return pltpu.CompilerParams(dimension_semantics=sem, vmem_limit_bytes=VMEM_LIMIT_BYTES)


def _rms(x):
    return x * lax.rsqrt(jnp.mean(x * x, axis=-1, keepdims=True) + EPS)


def _gelu(x):
    return 0.5 * x * (1.0 + lax.erf(x * (2.0 ** -0.5)))


def _silu(x):
    return x * jax.nn.sigmoid(x)


def _mm(a, b, dims=NN):
    assert a.dtype == b.dtype
    precision = lax.Precision.HIGHEST if a.dtype == F32 else None
    return lax.dot_general(a, b, dims, precision=precision, preferred_element_type=F32)


def _norm_mm_body(x_ref, g_ref, w_ref, o_ref, h_ref):
    @pl.when(pl.program_id(1) == 0)
    def _():
        h_ref[...] = (_rms(x_ref[...]) * g_ref[...]).astype(h_ref.dtype)

    o_ref[...] = _mm(h_ref[...], w_ref[...])


def norm_mm(x, g, w, groups, tm, tn):
    M, K = x.shape
    N = w.shape[1]
    ng = N // groups
    npg = ng // tn
    return pl.pallas_call(
        _norm_mm_body,
        grid=(M // tm, N // tn),
        in_specs=[
            pl.BlockSpec((tm, K), lambda i, j: (i, 0)),
            pl.BlockSpec((1, K), lambda i, j: (0, 0)),
            pl.BlockSpec((K, tn), lambda i, j: (0, j)),
        ],
        out_specs=pl.BlockSpec((None, tm, tn), lambda i, j: (j // npg, i, j % npg)),
        out_shape=jax.ShapeDtypeStruct((groups, M, ng), F32),
        scratch_shapes=[pltpu.VMEM((tm, K), w.dtype)],
        compiler_params=_cparams("parallel", "arbitrary"),
        name="norm_mm",
    )(x, g.reshape(1, K), w)


def _norm_qkv_body(x_ref, g_ref, w_ref, *rest, q_scale, n_heads, n_alias):
    qkv_ref, k_ref, v_ref, h_ref = rest[n_alias:]
    grp = pl.program_id(1)

    @pl.when(grp == 0)
    def _():
        h_ref[...] = (_rms(x_ref[...]) * g_ref[...]).astype(h_ref.dtype)

    o = _mm(h_ref[...], w_ref[...])
    qkv_ref[...] = (o * jnp.where(grp == 0, q_scale, 1.0)).astype(qkv_ref.dtype)

    def heads_out(ref):
        for h in range(n_heads):
            ref[:, h, :] = o[:, h * HEAD:(h + 1) * HEAD]

    pl.when(grp == 1)(lambda: heads_out(k_ref))
    pl.when(grp == 2)(lambda: heads_out(v_ref))


def norm_qkv(x, g, w, q_scale, layer_slot, n_slots, kv_prev, tm):
    M, K = x.shape
    W = w.shape[1] // 3
    nh = W // HEAD
    kv_shape = jax.ShapeDtypeStruct((n_slots, M, nh, HEAD), F32)
    kv_spec = pl.BlockSpec((None, tm, nh, HEAD), lambda i, j: (layer_slot, i, 0, 0))
    n_alias = 0 if kv_prev is None else 2
    return pl.pallas_call(
        functools.partial(_norm_qkv_body, q_scale=q_scale, n_heads=nh, n_alias=n_alias),
        grid=(M // tm, 3),
        in_specs=[
            pl.BlockSpec((tm, K), lambda i, j: (i, 0)),
            pl.BlockSpec((1, K), lambda i, j: (0, 0)),
            pl.BlockSpec((K, W), lambda i, j: (0, j)),
        ] + [pl.BlockSpec(memory_space=pl.ANY)] * n_alias,
        out_specs=[pl.BlockSpec((None, tm, W), lambda i, j: (j, i, 0)), kv_spec, kv_spec],
        out_shape=[jax.ShapeDtypeStruct((3, M, W), BF16), kv_shape, kv_shape],
        input_output_aliases={3: 1, 4: 2} if n_alias else {},
        scratch_shapes=[pltpu.VMEM((tm, K), w.dtype)],
        compiler_params=_cparams("parallel", "arbitrary"),
        name="norm_qkv",
    )(x, g.reshape(1, K), w, *(kv_prev or ()))


def _mm_res_body(a_ref, w_ref, r_ref, o_ref):
    o_ref[...] = r_ref[...] + _mm(a_ref[...], w_ref[...])


def mm_res(a, w, res, tm, tn):
    M, K = a.shape
    N = w.shape[1]
    tn = min(tn, N)
    return pl.pallas_call(
        _mm_res_body,
        grid=(M // tm, N // tn),
        in_specs=[
            pl.BlockSpec((tm, K), lambda i, j: (i, 0)),
            pl.BlockSpec((K, tn), lambda i, j: (0, j)),
            pl.BlockSpec((tm, tn), lambda i, j: (i, j)),
        ],
        out_specs=pl.BlockSpec((tm, tn), lambda i, j: (i, j)),
        out_shape=jax.ShapeDtypeStruct((M, N), F32),
        compiler_params=_cparams("parallel", "arbitrary"),
        name="mm_res",
    )(a, w, res)


def _even_core_body(*refs, n_heads, n_groups, gch, rows_in, has_s0, emit_vb):
    it = iter(refs)
    p_ref, cos_ref, sin_ref, dmask_ref, qdec_ref, kdec_ref, sgg_ref, ws_ref, sgb_ref = (next(it) for _ in range(9))
    s0_ref = next(it) if has_s0 else None
    y_ref, sout_ref = next(it), next(it)
    vb_ref = next(it) if emit_vb else None
    s_ref = next(it)
    C = HEAD
    cd = y_ref.dtype
    c = pl.program_id(1)
    w_ret = n_heads * HEAD

    @pl.when(c == 0)
    def _():
        if has_s0:
            s_ref[...] = s0_ref[...]
        else:
            s_ref[...] = jnp.zeros_like(s_ref)

    def rows(x):
        if rows_in == C:
            return x
        return jnp.concatenate([x, jnp.zeros((C - rows_in, x.shape[1]), x.dtype)], axis=0)

    cos = cos_ref[...]
    sin = sin_ref[...]

    def rot(x):
        return x * cos + pltpu.roll(x, HEAD // 2, 1) * sin

    for h in range(n_heads):
        sl = slice(h * HEAD, (h + 1) * HEAD)
        q = rot(rows(p_ref[0, :, sl]))
        k = rot(rows(p_ref[1, :, sl])) * (HEAD ** -0.5)
        v = rows(p_ref[2, :, sl]).astype(cd)
        g = rows(p_ref[3, :, sl])
        intra = _mm(q.astype(cd), k.astype(cd), NT) * dmask_ref[h]
        o = _mm(intra.astype(cd), v)
        s = s_ref[h]
        o = o + _mm((q * qdec_ref[h]).astype(cd), s.astype(cd))
        kv = _mm((k * kdec_ref[h]).astype(cd), v, TN)
        s_ref[h] = gch[h] * s + kv
        oa = _rms(o) * _silu(g)
        y_ref[:, sl] = oa[:rows_in].astype(y_ref.dtype)

    u = _gelu(rows(p_ref[4]))
    vg = _gelu(rows(p_ref[5]))
    xc = vg - jnp.mean(vg, axis=-1, keepdims=True)
    vb = xc * lax.rsqrt(jnp.mean(xc * xc, axis=-1, keepdims=True) + EPS) * sgg_ref[...]
    if emit_vb:
        vb_ref[...] = vb[:rows_in]
    row = lax.broadcasted_iota(jnp.int32, (C, C), 0)
    col = lax.broadcasted_iota(jnp.int32, (C, C), 1)
    for gi in range(n_groups):
        sl = slice(gi * HEAD, (gi + 1) * HEAD)
        w = jnp.where(col <= row, ws_ref[gi], 0.0).astype(cd)
        f = _mm(w, vb[:, sl].astype(cd)) + sgb_ref[gi]
        ob = u[:, sl] * f
        y_ref[:, w_ret + gi * HEAD:w_ret + (gi + 1) * HEAD] = ob[:rows_in].astype(y_ref.dtype)

    @pl.when(c == pl.num_programs(1) - 1)
    def _():
        sout_ref[...] = s_ref[...]


def even_core(proj, B, L, rows_in, cos, sin, dmask, qdec, kdec, gch, sg_g, w_s, sgb, s0, emit_vb, out_dtype):
    n_heads = dmask.shape[0]
    n_groups = w_s.shape[0]
    W = proj.shape[2]
    nc = L // rows_in
    has_s0 = s0 is not None
    const3 = lambda b, c: (0, 0, 0)
    in_specs = [
        pl.BlockSpec((6, rows_in, W), lambda b, c: (0, b * nc + c, 0)),
        pl.BlockSpec((HEAD, HEAD), lambda b, c: (c, 0)),
        pl.BlockSpec((HEAD, HEAD), lambda b, c: (c, 0)),
        pl.BlockSpec(dmask.shape, const3),
        pl.BlockSpec(qdec.shape, const3),
        pl.BlockSpec(kdec.shape, const3),
        pl.BlockSpec((1, W), lambda b, c: (0, 0)),
        pl.BlockSpec(w_s.shape, const3),
        pl.BlockSpec(sgb.shape, const3),
    ]
    args = [proj, cos, sin, dmask, qdec, kdec, sg_g.reshape(1, W), w_s, sgb]
    if has_s0:
        in_specs.append(pl.BlockSpec((None, n_heads, HEAD, HEAD), lambda b, c: (b, 0, 0, 0)))
        args.append(s0)
    out_specs = [
        pl.BlockSpec((rows_in, 2 * W), lambda b, c: (b * nc + c, 0)),
        pl.BlockSpec((None, n_heads, HEAD, HEAD), lambda b, c: (b, 0, 0, 0)),
    ]
    out_shape = [
        jax.ShapeDtypeStruct((B * L, 2 * W), out_dtype),
        jax.ShapeDtypeStruct((B, n_heads, HEAD, HEAD), F32),
    ]
    if emit_vb:
        out_specs.append(pl.BlockSpec((rows_in, W), lambda b, c: (b * nc + c, 0)))
        out_shape.append(jax.ShapeDtypeStruct((B * L, W), F32))
    body = functools.partial(_even_core_body, n_heads=n_heads, n_groups=n_groups, gch=gch,
                             rows_in=rows_in, has_s0=has_s0, emit_vb=emit_vb)
    return pl.pallas_call(
        body,
        grid=(B, nc),
        in_specs=in_specs,
        out_specs=out_specs,
        out_shape=out_shape,
        scratch_shapes=[pltpu.VMEM((n_heads, HEAD, HEAD), F32)],
        compiler_params=_cparams("parallel", "arbitrary"),
        name="even_core",
    )(*args)


def _mem_kv_body(m_ref, wk_ref, wv_ref, kg_ref, k_ref, v_ref, k16_ref, v16_ref, *, n_heads):
    m = m_ref[...].astype(BF16)
    k = jnp.dot(m, wk_ref[...], preferred_element_type=F32)
    kg = kg_ref[...]
    v = jnp.dot(m, wv_ref[...], preferred_element_type=F32)
    for h in range(n_heads):
        sl = slice(h * HEAD, (h + 1) * HEAD)
        kn = _rms(k[:, sl]) * kg
        k_ref[:, h, :] = kn
        v_ref[:, h, :] = v[:, sl]
        k16_ref[h] = kn.astype(BF16)
        v16_ref[h] = v[:, sl].astype(BF16)


def mem_kv(mem, w_k, w_v, k_g):
    B, NM, D = mem.shape
    depth, _, WX = w_k.shape
    nh = WX // HEAD
    out = jax.ShapeDtypeStruct((depth, B, NM, nh, HEAD), F32)
    out16 = jax.ShapeDtypeStruct((depth, B, nh, NM, HEAD), BF16)
    return pl.pallas_call(
        functools.partial(_mem_kv_body, n_heads=WX // HEAD),
        grid=(depth, B),
        in_specs=[
            pl.BlockSpec((None, NM, D), lambda l, b: (b, 0, 0)),
            pl.BlockSpec((None, D, WX), lambda l, b: (l, 0, 0)),
            pl.BlockSpec((None, D, WX), lambda l, b: (l, 0, 0)),
            pl.BlockSpec((None, 1, HEAD), lambda l, b: (l, 0, 0)),
        ],
        out_specs=[pl.BlockSpec((None, None, NM, nh, HEAD), lambda l, b: (l, b, 0, 0, 0))] * 2
        + [pl.BlockSpec((None, None, nh, NM, HEAD), lambda l, b: (l, b, 0, 0, 0))] * 2,
        out_shape=[out, out, out16, out16],
        compiler_params=_cparams("parallel", "parallel"),
        name="mem_kv",
    )(mem, w_k, w_v, k_g.reshape(depth, 1, HEAD))


def _mem_attention(q, qg, mk_ref, mv_ref, cd):
    outs = []
    for h in range(mk_ref.shape[0]):
        qn = (_rms(q[:, h * HEAD:(h + 1) * HEAD]) * qg).astype(cd)
        s = _mm(qn, mk_ref[h], NT)
        e = jnp.exp2(s - jnp.max(s, axis=-1, keepdims=True))
        outs.append(_mm(e.astype(cd), mv_ref[h]) * (1.0 / jnp.sum(e, axis=-1, keepdims=True)))
    return jnp.concatenate(outs, axis=-1).astype(cd)


def _cross_attn_body(x_ref, g_ref, wq_ref, qg_ref, mk_ref, mv_ref, wo_ref, o_ref):
    cd = wq_ref.dtype
    x = x_ref[...]
    q = _mm((_rms(x) * g_ref[...]).astype(cd), wq_ref[...])
    o_ref[...] = x + _mm(_mem_attention(q, qg_ref[...], mk_ref, mv_ref, cd), wo_ref[...])


def _mem_attn_body(q_ref, qg_ref, mk_ref, mv_ref, o_ref):
    o_ref[...] = _mem_attention(q_ref[...], qg_ref[...], mk_ref, mv_ref, o_ref.dtype)


def mem_attn(q, qg, mk_all, mv_all, layer):
    B, L, WX = q.shape
    nh, NM = mk_all.shape[2], mk_all.shape[3]
    mem_spec = pl.BlockSpec((None, None, nh, NM, HEAD), lambda b: (layer, b, 0, 0, 0))
    return pl.pallas_call(
        _mem_attn_body,
        grid=(B,),
        in_specs=[
            pl.BlockSpec((None, L, WX), lambda b: (b, 0, 0)),
            pl.BlockSpec((1, HEAD), lambda b: (0, 0)),
            mem_spec,
            mem_spec,
        ],
        out_specs=pl.BlockSpec((None, L, WX), lambda b: (b, 0, 0)),
        out_shape=jax.ShapeDtypeStruct((B, L, WX), mk_all.dtype),
        compiler_params=_cparams("parallel"),
        name="mem_attn",
    )(q, (qg * (LOG2E * HEAD ** -0.5)).reshape(1, HEAD), mk_all, mv_all)


def cross_attn(x, g, wq, qg, mk_all, mv_all, layer, wo, tm):
    B, L, D = x.shape
    nh, NM = mk_all.shape[2], mk_all.shape[3]
    WX = nh * HEAD
    qg = qg * (LOG2E * HEAD ** -0.5)
    mem_spec = pl.BlockSpec((None, None, nh, NM, HEAD), lambda b, i: (layer, b, 0, 0, 0))
    return pl.pallas_call(
        _cross_attn_body,
        grid=(B, L // tm),
        in_specs=[
            pl.BlockSpec((None, tm, D), lambda b, i: (b, i, 0)),
            pl.BlockSpec((1, D), lambda b, i: (0, 0)),
            pl.BlockSpec((D, WX), lambda b, i: (0, 0)),
            pl.BlockSpec((1, HEAD), lambda b, i: (0, 0)),
            mem_spec,
            mem_spec,
            pl.BlockSpec((WX, D), lambda b, i: (0, 0)),
        ],
        out_specs=pl.BlockSpec((None, tm, D), lambda b, i: (b, i, 0)),
        out_shape=jax.ShapeDtypeStruct((B, L, D), F32),
        compiler_params=_cparams("parallel", "parallel"),
        name="cross_attn",
    )(x, g.reshape(1, D), wq, qg.reshape(1, HEAD), mk_all, mv_all, wo)


def _ffn_body(x_ref, g_ref, wg_ref, wu_ref, wd_ref, o_ref, h_ref, acc_ref):
    f = pl.program_id(1)

    @pl.when(f == 0)
    def _():
        h_ref[...] = (_rms(x_ref[...]) * g_ref[...]).astype(h_ref.dtype)
        acc_ref[...] = jnp.zeros_like(acc_ref)

    h = h_ref[...]
    a = _silu(_mm(h, wg_ref[...])) * _mm(h, wu_ref[...])
    acc_ref[...] += _mm(a.astype(h.dtype), wd_ref[...])

    @pl.when(f == pl.num_programs(1) - 1)
    def _():
        o_ref[...] = x_ref[...] + acc_ref[...]


def ffn(x, g, w_gu, w_down, tm, tf):
    M, D = x.shape
    F = w_down.shape[0]
    nf = F // tf
    return pl.pallas_call(
        _ffn_body,
        grid=(M // tm, nf),
        in_specs=[
            pl.BlockSpec((tm, D), lambda i, f: (i, 0)),
            pl.BlockSpec((1, D), lambda i, f: (0, 0)),
            pl.BlockSpec((D, tf), lambda i, f: (0, f)),
            pl.BlockSpec((D, tf), lambda i, f: (0, nf + f)),
            pl.BlockSpec((tf, D), lambda i, f: (f, 0)),
        ],
        out_specs=pl.BlockSpec((tm, D), lambda i, f: (i, 0)),
        out_shape=jax.ShapeDtypeStruct((M, D), F32),
        scratch_shapes=[pltpu.VMEM((tm, D), w_gu.dtype), pltpu.VMEM((tm, D), F32)],
        compiler_params=_cparams("parallel", "arbitrary"),
        name="ffn",
    )(x, g.reshape(1, D), w_gu, w_gu, w_down)


def _router_body(x_ref, g_ref, wr_ref, o_ref, *, n_experts):
    h = _rms(x_ref[...]) * g_ref[...]
    logits = jnp.dot(h, wr_ref[...], preferred_element_type=F32, precision=lax.Precision.HIGHEST)
    lane = lax.broadcasted_iota(jnp.int32, logits.shape, 1).astype(F32)
    neg = jnp.float32(-jnp.inf)
    logits = jnp.where(lane < n_experts, logits, neg)
    m1 = jnp.max(logits, axis=-1, keepdims=True)
    i1 = jnp.min(jnp.where(logits == m1, lane, float(LANES)), axis=-1, keepdims=True)
    rest = jnp.where(lane == i1, neg, logits)
    m2 = jnp.max(rest, axis=-1, keepdims=True)
    i2 = jnp.min(jnp.where(rest == m2, lane, float(LANES)), axis=-1, keepdims=True)
    e2 = jnp.exp(m2 - m1)
    den = 1.0 + e2
    o_ref[...] = jnp.where(lane == i1, 1.0 / den, 0.0) + jnp.where(lane == i2, e2 / den, 0.0)


def router(x, g, w_router_pad, n_experts, tm):
    M, D = x.shape
    return pl.pallas_call(
        functools.partial(_router_body, n_experts=n_experts),
        grid=(M // tm,),
        in_specs=[
            pl.BlockSpec((tm, D), lambda i: (i, 0)),
            pl.BlockSpec((1, D), lambda i: (0, 0)),
            pl.BlockSpec((D, LANES), lambda i: (0, 0)),
        ],
        out_specs=pl.BlockSpec((tm, LANES), lambda i: (i, 0)),
        out_shape=jax.ShapeDtypeStruct((M, LANES), F32),
        compiler_params=_cparams("parallel"),
        name="router",
    )(x, g.reshape(1, D), w_router_pad)


def _moe_body(x_ref, g_ref, gate_ref, wg_ref, wu_ref, wd_ref, o_ref, h_ref, acc_ref):
    e = pl.program_id(1)

    @pl.when(e == 0)
    def _():
        h_ref[...] = (_rms(x_ref[...]) * g_ref[...]).astype(BF16)
        acc_ref[...] = jnp.zeros_like(acc_ref)

    h = h_ref[...]
    a = _silu(jnp.dot(h, wg_ref[...], preferred_element_type=F32)) * jnp.dot(h, wu_ref[...], preferred_element_type=F32)
    gates = gate_ref[...]
    lane = lax.broadcasted_iota(jnp.int32, gates.shape, 1)
    gate = jnp.sum(jnp.where(lane == e, gates, 0.0), axis=-1, keepdims=True)
    acc_ref[...] += gate * jnp.dot(a.astype(BF16), wd_ref[...], preferred_element_type=F32)

    @pl.when(e == pl.num_programs(1) - 1)
    def _():
        o_ref[...] = x_ref[...] + acc_ref[...]


def moe_dense(x, g, gates, w_gu, w_down, tm):
    M, D = x.shape
    E, F, _ = w_down.shape
    return pl.pallas_call(
        _moe_body,
        grid=(M // tm, E),
        in_specs=[
            pl.BlockSpec((tm, D), lambda i, e: (i, 0)),
            pl.BlockSpec((1, D), lambda i, e: (0, 0)),
            pl.BlockSpec((tm, LANES), lambda i, e: (i, 0)),
            pl.BlockSpec((None, D, F), lambda i, e: (e, 0, 0)),
            pl.BlockSpec((None, D, F), lambda i, e: (e, 0, 1)),
            pl.BlockSpec((None, F, D), lambda i, e: (e, 0, 0)),
        ],
        out_specs=pl.BlockSpec((tm, D), lambda i, e: (i, 0)),
        out_shape=jax.ShapeDtypeStruct((M, D), F32),
        scratch_shapes=[pltpu.VMEM((tm, D), BF16), pltpu.VMEM((tm, D), F32)],
        compiler_params=_cparams("parallel", "arbitrary"),
        name="moe",
    )(x, g.reshape(1, D), gates, w_gu, w_gu, w_down)


def _log2_terms(z2):
    sign_bit = jnp.int32(-2 ** 31)
    neg_abs = lax.bitcast_convert_type(lax.bitcast_convert_type(z2, jnp.int32) | sign_bit, F32)
    nlom = jnp.maximum(z2, 0.0) + jnp.log2(1.0 + jnp.exp2(neg_abs))
    return nlom, z2 - nlom


MASKED_LOG2 = -1e30


def _sb_prompt_body(q_ref, k_ref, v_ref, b_ref, u_ref, o_ref, acc_ref, r_ref, lom_ref, lb_ref, *, tk, rq):
    i = pl.program_id(2)
    tq = rq * tk
    n = rq * (i + 1)
    bias = b_ref[...]
    acc_ref[...] = jnp.zeros_like(acc_ref)
    r_ref[...] = jnp.zeros_like(r_ref)

    def keys(ref, kb):
        return ref[pl.ds(pl.multiple_of(kb * tk, tk), tk), :]

    def stage_a(kb, slot, masked):
        z = lax.dot_general(q_ref[...], keys(k_ref, kb), NT, preferred_element_type=F32) + bias
        nlom, lbeta = _log2_terms(z)
        if masked:
            row = lax.broadcasted_iota(jnp.int32, (tq, tk), 0)
            col = lax.broadcasted_iota(jnp.int32, (tq, tk), 1)
            keep = col + kb * tk < row + i * tq
            nlom = jnp.where(keep, nlom, 0.0)
            lbeta = jnp.where(keep, lbeta, MASKED_LOG2)
        lom_ref[slot] = nlom.astype(BF16)
        lb_ref[slot] = lbeta

    def stage_b(kb, slot):
        c = jnp.dot(lom_ref[slot], u_ref[...], preferred_element_type=F32)
        r = r_ref[...]
        a = jnp.exp2(lb_ref[slot] - c[:, :tk] - jnp.concatenate([r] * (tk // LANES), axis=1))
        acc_ref[...] += jnp.dot(a.astype(BF16), keys(v_ref, kb), preferred_element_type=F32)
        r_ref[...] = r + c[:, tk:]

    stage_a(n - 1, 0, True)
    for t in range(1, rq):
        stage_a(n - 1 - t, t % 2, True)
        stage_b(n - t, (t - 1) % 2)

    def body(p, carry):
        kb = n - 1 - rq - 2 * p
        stage_a(kb, 0, False)
        stage_b(kb + 1, 1)
        stage_a(kb - 1, 1, False)
        stage_b(kb, 0)
        return carry

    lax.fori_loop(0, (n - rq) // 2, body, 0)
    stage_b(0, 1)
    o_ref[...] = acc_ref[...].astype(o_ref.dtype)


def _suffix_matrix(t):
    j = np.arange(t)[:, None]
    s = np.arange(t)[None, :]
    u = np.concatenate([(j > s).astype(np.float32), np.ones((t, LANES), np.float32)], axis=1)
    return jnp.asarray(u, dtype=BF16)


def sb_prompt(qkv, bias, B, S, tk, rq):
    W = qkv.shape[2]
    H = W // HEAD
    assert rq % 2 == 0
    tq = rq * tk
    nq = S // tq
    bias_b = jnp.broadcast_to((bias.astype(F32) * LOG2E)[:, None, None], (H, 1, tk))
    return pl.pallas_call(
        functools.partial(_sb_prompt_body, tk=tk, rq=rq),
        grid=(B, H, nq),
        in_specs=[
            pl.BlockSpec((None, tq, HEAD), lambda b, h, i: (0, b * nq + i, h)),
            pl.BlockSpec((None, S, HEAD), lambda b, h, i: (1, b, h)),
            pl.BlockSpec((None, S, HEAD), lambda b, h, i: (2, b, h)),
            pl.BlockSpec((None, 1, tk), lambda b, h, i: (h, 0, 0)),
            pl.BlockSpec((tk, tk + LANES), lambda b, h, i: (0, 0)),
        ],
        out_specs=pl.BlockSpec((tq, HEAD), lambda b, h, i: (b * nq + i, h)),
        out_shape=jax.ShapeDtypeStruct((B * S, W), BF16),
        scratch_shapes=[pltpu.VMEM((tq, HEAD), F32), pltpu.VMEM((tq, LANES), F32),
                        pltpu.VMEM((2, tq, tk), BF16), pltpu.VMEM((2, tq, tk), F32)],
        compiler_params=_cparams("parallel", "parallel", "arbitrary"),
        name="sb_prompt",
    )(qkv, qkv, qkv, bias_b, _suffix_matrix(tk))


def _sb_sample_body(pt_ref, q_ref, kn_ref, vn_ref, b_ref, u_ref, *rest, n_heads, ppb, t_valid):
    kp_refs = rest[:ppb]
    vp_refs = rest[ppb:2 * ppb]
    o_ref, qx_ref, acc_ref, r_ref, nl_ref, lb_ref, cs_ref = rest[2 * ppb:]
    del pt_ref
    pg = pl.program_id(1)
    G = LANES // n_heads
    bias = b_ref[...]

    def stage_a(get_k, npages, masked):
        z = bias
        for h in range(n_heads):
            k_h = jnp.concatenate([get_k(n, h) for n in range(npages)], axis=0)
            z = z + lax.dot_general(k_h, qx_ref[h], NT, preferred_element_type=F32)
        nlom, lbeta = _log2_terms(z)
        if masked:
            col = lax.broadcasted_iota(jnp.int32, z.shape, 1)
            row = lax.broadcasted_iota(jnp.int32, z.shape, 0)
            t = col & (G - 1)
            keep = (row < t) & (t < t_valid)
            nlom = jnp.where(keep, nlom, 0.0)
            lbeta = jnp.where(keep, lbeta, MASKED_LOG2)
        nlom_p = [nlom[n * PAGE:(n + 1) * PAGE] for n in range(npages)]
        sums = jnp.concatenate([jnp.sum(x, axis=0, keepdims=True) for x in nlom_p], axis=0)
        return jnp.concatenate(nlom_p, axis=1).astype(BF16), lbeta, sums

    def stage_b(nl, lbeta, sums, get_v, npages):
        after = jnp.dot(u_ref[...], nl, preferred_element_type=F32)
        r = r_ref[...]
        a_t = []
        for n in range(npages):
            a = jnp.exp2(lbeta[n * PAGE:(n + 1) * PAGE] - after[:, n * LANES:(n + 1) * LANES] - r)
            a_t.append(a.T.astype(BF16))
            r = r + sums[n:n + 1]
        r_ref[...] = r
        a_t = jnp.concatenate(a_t, axis=1)
        for h in range(n_heads):
            v_h = jnp.concatenate([get_v(n, h) for n in range(npages)], axis=0)
            acc_ref[h] += jnp.dot(a_t[h * G:(h + 1) * G], v_h, preferred_element_type=F32)

    def pad_rows(x, before, total):
        parts = []
        if before:
            parts.append(jnp.zeros((before, x.shape[1]), x.dtype))
        parts.append(x)
        if total - before - x.shape[0]:
            parts.append(jnp.zeros((total - before - x.shape[0], x.shape[1]), x.dtype))
        return jnp.concatenate(parts, axis=0)

    @pl.when(pg == 0)
    def _():
        for h in range(n_heads):
            sl = slice(h * HEAD, (h + 1) * HEAD)
            qx_ref[h] = pad_rows(q_ref[:, sl] * (LOG2E * HEAD ** -0.5), h * G, LANES).astype(BF16)
        acc_ref[...] = jnp.zeros_like(acc_ref)
        r_ref[...] = jnp.zeros_like(r_ref)

        def new_rows(ref):
            return lambda n, h: pad_rows(ref[:, h * HEAD:(h + 1) * HEAD], 0, PAGE).astype(BF16)

        stage_b(*stage_a(new_rows(kn_ref), 1, True), new_rows(vn_ref), 1)
        nl_ref[1] = jnp.zeros(nl_ref.shape[1:], nl_ref.dtype)
        lb_ref[1] = jnp.full(lb_ref.shape[1:], MASKED_LOG2, lb_ref.dtype)
        cs_ref[1] = jnp.zeros(cs_ref.shape[1:], cs_ref.dtype)

    def pool_rows(refs):
        return lambda n, h: refs[n][pl.ds(h, PAGE, stride=n_heads), :].astype(BF16)

    done = lax.rem(pg + 1, 2)
    stage_b(nl_ref[done], lb_ref[done], cs_ref[done], pool_rows(vp_refs), ppb)
    cur = lax.rem(pg, 2)
    nl, lbeta, sums = stage_a(pool_rows(kp_refs), ppb, False)
    nl_ref[cur] = nl
    lb_ref[cur] = lbeta
    cs_ref[cur] = sums

    @pl.when(pg == pl.num_programs(1) - 1)
    def _():
        for h in range(n_heads):
            o_ref[:, h * HEAD:(h + 1) * HEAD] = acc_ref[h][:T_PAD].astype(o_ref.dtype)


def sb_sample(proj, bias, pool_k, pool_v, layer, page_table, t_valid, ppb, out_dtype):
    W = proj.shape[2]
    H = W // HEAD
    G = LANES // H
    B, n_pages = page_table.shape
    npg = n_pages // ppb
    bias_b = jnp.repeat(bias.astype(F32) * LOG2E, G).reshape(1, LANES)
    j = np.arange(PAGE)
    u_t = jnp.asarray((j[None, :] > j[:, None]).astype(np.float32), dtype=BF16)

    def rows_view(pool):
        return pool.reshape(pool.shape[0], pool.shape[1], PAGE * H, HEAD)

    def tok(g):
        return pl.BlockSpec((None, T_PAD, W), lambda b, p, pt: (g, b, 0))

    def pool_spec(n, lag):
        def index(b, p, pt):
            grp = jnp.clip(p - lag, 0, npg - 1)
            return layer, pt[b, n_pages - 1 - (grp * ppb + n)], 0, 0

        return pl.BlockSpec((None, None, PAGE * H, HEAD), index)

    grid_spec = pltpu.PrefetchScalarGridSpec(
        num_scalar_prefetch=1,
        grid=(B, npg + 1),
        in_specs=[tok(0), tok(1), tok(2),
                  pl.BlockSpec((1, LANES), lambda b, p, pt: (0, 0)),
                  pl.BlockSpec((PAGE, PAGE), lambda b, p, pt: (0, 0))]
        + [pool_spec(n, 0) for n in range(ppb)] + [pool_spec(n, 1) for n in range(ppb)],
        out_specs=pl.BlockSpec((T_PAD, W), lambda b, p, pt: (b, 0)),
        scratch_shapes=[pltpu.VMEM((H, LANES, HEAD), BF16), pltpu.VMEM((H, G, HEAD), F32),
                        pltpu.VMEM((1, LANES), F32),
                        pltpu.VMEM((2, PAGE, ppb * LANES), BF16), pltpu.VMEM((2, ppb * PAGE, LANES), F32),
                        pltpu.VMEM((2, ppb, LANES), F32)],
    )
    return pl.pallas_call(
        functools.partial(_sb_sample_body, n_heads=H, ppb=ppb, t_valid=t_valid),
        grid_spec=grid_spec,
        out_shape=jax.ShapeDtypeStruct((B * T_PAD, W), out_dtype),
        compiler_params=_cparams("parallel", "arbitrary"),
        name="sb_sample",
    )(page_table, proj, proj, proj, bias_b, u_t, *([rows_view(pool_k)] * ppb), *([rows_view(pool_v)] * ppb))


def _rope_tables(pos):
    half = HEAD // 2
    inv = 1.0 / (ROPE_BASE ** jnp.linspace(0.0, 1.0, half, dtype=F32))
    ang = pos[:, None] * inv[None, :]
    cos, sin = jnp.cos(ang), jnp.sin(ang)
    return jnp.concatenate([cos, cos], axis=-1), jnp.concatenate([-sin, sin], axis=-1)


def _retention_tables(n_heads, chunk):
    log_g = np.log(1.0 - 2.0 ** (-5.0 - np.arange(n_heads, dtype=np.float64)))
    idx = np.arange(HEAD, dtype=np.float64)
    diff = idx[:, None] - idx[None, :]
    dmask = np.where(diff >= 0, np.exp(log_g[:, None, None] * np.maximum(diff, 0.0)), 0.0)
    qdec = np.exp(log_g[:, None] * (idx + 1.0)[None, :])
    kdec = np.where(idx[None, :] < chunk, np.exp(log_g[:, None] * (chunk - 1.0 - idx)[None, :]), 0.0)
    bc = lambda t: jnp.asarray(np.broadcast_to(t[:, :, None], (n_heads, HEAD, HEAD)), dtype=F32)
    gch = tuple(float(x) for x in np.exp(log_g * chunk))
    return jnp.asarray(dmask, dtype=F32), bc(qdec), bc(kdec), gch


def kernel(x_prompt, x_sample, mem_prompt, state_ret, cache_sb_k, cache_sb_v, page_table, cache_mem_k, cache_mem_v, norm_mix_g, norm_x_g, norm_ffn_g, w_in_even, w_out_even, sg_norm_g, sg_w_s, sg_b, w_qkv_sb, w_o_sb, sb_bias, w_q_x, w_k_x, w_v_x, w_o_x, q_norm_x_g, k_norm_x_g, w_gu_dense, w_down_dense, w_router, w_gu_moe, w_down_moe):
    B, S, D = x_prompt.shape
    BS, T, _ = x_sample.shape
    depth = norm_mix_g.shape[0]
    n_heads_ret = state_ret.shape[2]
    n_experts = w_router.shape[2]
    past = page_table.shape[1] * PAGE
    H_sb = cache_sb_k.shape[3]
    W_sb = H_sb * HEAD
    W_x = w_q_x.shape[2]

    bf = lambda w: w.astype(BF16)
    TM = min(1024, B * S)
    MS = BS * T_PAD

    xp = x_prompt.reshape(B * S, D)
    xs = jnp.pad(x_sample, ((0, 0), (0, T_PAD - T), (0, 0))).reshape(MS, D)

    cos_p, sin_p = _rope_tables(jnp.arange(S, dtype=F32))
    cos_s, sin_s = _rope_tables(past + jnp.arange(HEAD, dtype=F32))
    dmask, qdec, kdec_p, gch_p = _retention_tables(n_heads_ret, HEAD)
    _, _, kdec_s, gch_s = _retention_tables(n_heads_ret, T)

    mk_p, mv_p, mk_p16, mv_p16 = mem_kv(mem_prompt, bf(w_k_x), bf(w_v_x), k_norm_x_g)
    cmk = jnp.transpose(cache_mem_k, (0, 1, 3, 2, 4))
    cmv = jnp.transpose(cache_mem_v, (0, 1, 3, 2, 4))
    cmk16, cmv16 = bf(cmk), bf(cmv)

    ret_p, ret_s, sgv_s = [], [], []
    kv_p = None
    sbk_s, sbv_s = [], []
    for l in range(depth):
        i = l // 2
        sample_f32 = l + 2 < depth
        if l % 2 == 0:
            w_in, w_out = bf(w_in_even[i]), bf(w_out_even[i])
            sgb = jnp.broadcast_to(sg_b[i][:, :, None], sg_b[i].shape + (HEAD,))
            proj = norm_mm(xp, norm_mix_g[l], w_in, 6, TM, 512)
            y, sp = even_core(proj, B, S, HEAD, cos_p, sin_p, dmask, qdec, kdec_p, gch_p,
                              sg_norm_g[i], sg_w_s[i], sgb, None, False, BF16)
            xp = mm_res(y, w_out, xp, TM, 512)
            if sample_f32:
                w_in, w_out = w_in_even[i], w_out_even[i]
            proj = norm_mm(xs, norm_mix_g[l], w_in, 6, MS, 512)
            y, ss, vs = even_core(proj, BS, T_PAD, T_PAD, cos_s, sin_s, dmask, qdec, kdec_s, gch_s,
                                  sg_norm_g[i], sg_w_s[i], sgb, state_ret[i], True, w_in.dtype)
            xs = mm_res(y, w_out, xs, MS, 512)
            ret_p.append(sp)
            ret_s.append(ss)
            sgv_s.append(vs.reshape(BS, T_PAD, -1)[:, :T])
        else:
            w_qkv, w_o = bf(w_qkv_sb[i]), bf(w_o_sb[i])
            qkv16, *kv_p = norm_qkv(xp, norm_mix_g[l], w_qkv, LOG2E * HEAD ** -0.5, i, depth // 2, kv_p,
                                    min(512, B * S))
            o = sb_prompt(qkv16, sb_bias[i], B, S, 256, 2)
            xp = mm_res(o, w_o, xp, TM, 512)
            if sample_f32:
                w_qkv, w_o = w_qkv_sb[i], w_o_sb[i]
            proj = norm_mm(xs, norm_mix_g[l], w_qkv, 3, MS, 512)
            o = sb_sample(proj, sb_bias[i], cache_sb_k, cache_sb_v, i, page_table, T, math.gcd(8, page_table.shape[1]), w_o.dtype)
            xs = mm_res(o, w_o, xs, MS, 512)
            sbk_s.append(proj[1].reshape(BS, T_PAD, H_sb, HEAD)[:, :T])
            sbv_s.append(proj[2].reshape(BS, T_PAD, H_sb, HEAD)[:, :T])
        wq, wo = bf(w_q_x[l]), bf(w_o_x[l])
        xp = cross_attn(xp.reshape(B, S, D), norm_x_g[l], wq, q_norm_x_g[l], mk_p16, mv_p16, l, wo, 512).reshape(B * S, D)
        mem_s = (cmk16, cmv16)
        if sample_f32:
            wq, wo, mem_s = w_q_x[l], w_o_x[l], (cmk, cmv)
        q_s = norm_mm(xs, norm_x_g[l], wq, 1, MS, 512)
        o_s = mem_attn(q_s.reshape(BS, T_PAD, -1), q_norm_x_g[l], *mem_s, l)
        xs = mm_res(o_s.reshape(MS, -1), wo, xs, MS, 512)
        if l % 2 == 0:
            w_gu, w_down = bf(w_gu_dense[i]), bf(w_down_dense[i])
            xp = ffn(xp, norm_ffn_g[l], w_gu, w_down, TM, 256)
            if sample_f32:
                w_gu, w_down = w_gu_dense[i], w_down_dense[i]
            xs = ffn(xs, norm_ffn_g[l], w_gu, w_down, MS, 256)
        else:
            w_gu, w_down = bf(w_gu_moe[i]), bf(w_down_moe[i])
            wr = jnp.pad(w_router[i], ((0, 0), (0, LANES - n_experts)))

            def moe(x, tm):
                gates = router(x, norm_ffn_g[l], wr, n_experts, tm)
                return moe_dense(x, norm_ffn_g[l], gates, w_gu, w_down, tm)

            xp, xs = moe(xp, min(512, B * S)), moe(xs, MS)

    y_p = xp.reshape(B, S, D)
    y_s = xs.reshape(BS, T_PAD, D)[:, :T]
    return (y_p, y_s, jnp.stack(ret_p), jnp.stack(ret_s), jnp.stack(sgv_s),
            kv_p[0].reshape(-1, B, S, H_sb, HEAD), kv_p[1].reshape(-1, B, S, H_sb, HEAD),
            jnp.stack(sbk_s), jnp.stack(sbv_s), mk_p, mv_p)
```

```python
import functools
import math

import numpy as np
import jax
import jax.numpy as jnp
from jax import lax
from jax.experimental import pallas as pl
from jax.experimental.pallas import tpu as pltpu

F32 = jnp.float32
BF16 = jnp.bfloat16
EPS = 1e-6
ROPE_BASE = 10000.0
LOG2E = math.log2(math.e)
LANES = 128
VMEM_LIMIT_BYTES = 56 * 1024 * 1024
HEAD = 128
PAGE = 128
T_PAD = 8
NN = (((1,), (0,)), ((), ()))
NT = (((1,), (1,)), ((), ()))
TN = (((0,), (0,)), ((), ()))


def _cparams(*sem):
    return pltpu.CompilerParams(dimension_semantics=sem, vmem_limit_bytes=VMEM_LIMIT_BYTES)


def _rms(x):
    return x * lax.rsqrt(jnp.mean(x * x, axis=-1, keepdims=True) + EPS)


def _gelu(x):
    return 0.5 * x * (1.0 + lax.erf(x * (2.0 ** -0.5)))


def _silu(x):
    return x * jax.nn.sigmoid(x)


def _mm(a, b, dims=NN):
    assert a.dtype == b.dtype
    precision = lax.Precision.HIGHEST if a.dtype == F32 else None
    return lax.dot_general(a, b, dims, precision=precision, preferred_element_type=F32)


def _norm_mm_body(x_ref, g_ref, w_ref, o_ref, h_ref):
    @pl.when(pl.program_id(1) == 0)
    def _():
        h_ref[...] = (_rms(x_ref[...]) * g_ref[...]).astype(h_ref.dtype)

    o_ref[...] = _mm(h_ref[...], w_ref[...])


def norm_mm(x, g, w, groups, tm, tn):
    M, K = x.shape
    N = w.shape[1]
    ng = N // groups
    npg = ng // tn
    return pl.pallas_call(
        _norm_mm_body,
        grid=(M // tm, N // tn),
        in_specs=[
            pl.BlockSpec((tm, K), lambda i, j: (i, 0)),
            pl.BlockSpec((1, K), lambda i, j: (0, 0)),
            pl.BlockSpec((K, tn), lambda i, j: (0, j)),
        ],
        out_specs=pl.BlockSpec((None, tm, tn), lambda i, j: (j // npg, i, j % npg)),
        out_shape=jax.ShapeDtypeStruct((groups, M, ng), F32),
        scratch_shapes=[pltpu.VMEM((tm, K), w.dtype)],
        compiler_params=_cparams("parallel", "arbitrary"),
        name="norm_mm",
    )(x, g.reshape(1, K), w)


def _norm_qkv_body(x_ref, g_ref, w_ref, *rest, q_scale, n_heads, n_alias):
    qkv_ref, k_ref, v_ref, h_ref = rest[n_alias:]
    grp = pl.program_id(1)

    @pl.when(grp == 0)
    def _():
        h_ref[...] = (_rms(x_ref[...]) * g_ref[...]).astype(h_ref.dtype)

    o = _mm(h_ref[...], w_ref[...])
    qkv_ref[...] = (o * jnp.where(grp == 0, q_scale, 1.0)).astype(qkv_ref.dtype)

    def heads_out(ref):
        for h in range(n_heads):
            ref[:, h, :] = o[:, h * HEAD:(h + 1) * HEAD]

    pl.when(grp == 1)(lambda: heads_out(k_ref))
    pl.when(grp == 2)(lambda: heads_out(v_ref))


def norm_qkv(x, g, w, q_scale, layer_slot, n_slots, kv_prev, tm):
    M, K = x.shape
    W = w.shape[1] // 3
    nh = W // HEAD
    kv_shape = jax.ShapeDtypeStruct((n_slots, M, nh, HEAD), F32)
    kv_spec = pl.BlockSpec((None, tm, nh, HEAD), lambda i, j: (layer_slot, i, 0, 0))
    n_alias = 0 if kv_prev is None else 2
    return pl.pallas_call(
        functools.partial(_norm_qkv_body, q_scale=q_scale, n_heads=nh, n_alias=n_alias),
        grid=(M // tm, 3),
        in_specs=[
            pl.BlockSpec((tm, K), lambda i, j: (i, 0)),
            pl.BlockSpec((1, K), lambda i, j: (0, 0)),
            pl.BlockSpec((K, W), lambda i, j: (0, j)),
        ] + [pl.BlockSpec(memory_space=pl.ANY)] * n_alias,
        out_specs=[pl.BlockSpec((None, tm, W), lambda i, j: (j, i, 0)), kv_spec, kv_spec],
        out_shape=[jax.ShapeDtypeStruct((3, M, W), BF16), kv_shape, kv_shape],
        input_output_aliases={3: 1, 4: 2} if n_alias else {},
        scratch_shapes=[pltpu.VMEM((tm, K), w.dtype)],
        compiler_params=_cparams("parallel", "arbitrary"),
        name="norm_qkv",
    )(x, g.reshape(1, K), w, *(kv_prev or ()))


def _mm_res_body(a_ref, w_ref, r_ref, o_ref):
    o_ref[...] = r_ref[...] + _mm(a_ref[...], w_ref[...])


def mm_res(a, w, res, tm, tn):
    M, K = a.shape
    N = w.shape[1]
    tn = min(tn, N)
    return pl.pallas_call(
        _mm_res_body,
        grid=(M // tm, N // tn),
        in_specs=[
            pl.BlockSpec((tm, K), lambda i, j: (i, 0)),
            pl.BlockSpec((K, tn), lambda i, j: (0, j)),
            pl.BlockSpec((tm, tn), lambda i, j: (i, j)),
        ],
        out_specs=pl.BlockSpec((tm, tn), lambda i, j: (i, j)),
        out_shape=jax.ShapeDtypeStruct((M, N), F32),
        compiler_params=_cparams("parallel", "arbitrary"),
        name="mm_res",
    )(a, w, res)


def _even_core_body(*refs, n_heads, n_groups, gch, rows_in, has_s0, emit_vb):
    it = iter(refs)
    p_ref, cos_ref, sin_ref, dmask_ref, qdec_ref, kdec_ref, sgg_ref, ws_ref, sgb_ref = (next(it) for _ in range(9))
    s0_ref = next(it) if has_s0 else None
    y_ref, sout_ref = next(it), next(it)
    vb_ref = next(it) if emit_vb else None
    s_ref = next(it)
    C = HEAD
    cd = y_ref.dtype
    c = pl.program_id(1)
    w_ret = n_heads * HEAD

    @pl.when(c == 0)
    def _():
        if has_s0:
            s_ref[...] = s0_ref[...]
        else:
            s_ref[...] = jnp.zeros_like(s_ref)

    def rows(x):
        if rows_in == C:
            return x
        return jnp.concatenate([x, jnp.zeros((C - rows_in, x.shape[1]), x.dtype)], axis=0)

    cos = cos_ref[...]
    sin = sin_ref[...]

    def rot(x):
        return x * cos + pltpu.roll(x, HEAD // 2, 1) * sin

    for h in range(n_heads):
        sl = slice(h * HEAD, (h + 1) * HEAD)
        q = rot(rows(p_ref[0, :, sl]))
        k = rot(rows(p_ref[1, :, sl])) * (HEAD ** -0.5)
        v = rows(p_ref[2, :, sl]).astype(cd)
        g = rows(p_ref[3, :, sl])
        intra = _mm(q.astype(cd), k.astype(cd), NT) * dmask_ref[h]
        o = _mm(intra.astype(cd), v)
        s = s_ref[h]
        o = o + _mm((q * qdec_ref[h]).astype(cd), s.astype(cd))
        kv = _mm((k * kdec_ref[h]).astype(cd), v, TN)
        s_ref[h] = gch[h] * s + kv
        oa = _rms(o) * _silu(g)
        y_ref[:, sl] = oa[:rows_in].astype(y_ref.dtype)

    u = _gelu(rows(p_ref[4]))
    vg = _gelu(rows(p_ref[5]))
    xc = vg - jnp.mean(vg, axis=-1, keepdims=True)
    vb = xc * lax.rsqrt(jnp.mean(xc * xc, axis=-1, keepdims=True) + EPS) * sgg_ref[...]
    if emit_vb:
        vb_ref[...] = vb[:rows_in]
    row = lax.broadcasted_iota(jnp.int32, (C, C), 0)
    col = lax.broadcasted_iota(jnp.int32, (C, C), 1)
    for gi in range(n_groups):
        sl = slice(gi * HEAD, (gi + 1) * HEAD)
        w = jnp.where(col <= row, ws_ref[gi], 0.0).astype(cd)
        f = _mm(w, vb[:, sl].astype(cd)) + sgb_ref[gi]
        ob = u[:, sl] * f
        y_ref[:, w_ret + gi * HEAD:w_ret + (gi + 1) * HEAD] = ob[:rows_in].astype(y_ref.dtype)

    @pl.when(c == pl.num_programs(1) - 1)
    def _():
        sout_ref[...] = s_ref[...]


def even_core(proj, B, L, rows_in, cos, sin, dmask, qdec, kdec, gch, sg_g, w_s, sgb, s0, emit_vb, out_dtype):
    n_heads = dmask.shape[0]
    n_groups = w_s.shape[0]
    W = proj.shape[2]
    nc = L // rows_in
    has_s0 = s0 is not None
    const3 = lambda b, c: (0, 0, 0)
    in_specs = [
        pl.BlockSpec((6, rows_in, W), lambda b, c: (0, b * nc + c, 0)),
        pl.BlockSpec((HEAD, HEAD), lambda b, c: (c, 0)),
        pl.BlockSpec((HEAD, HEAD), lambda b, c: (c, 0)),
        pl.BlockSpec(dmask.shape, const3),
        pl.BlockSpec(qdec.shape, const3),
        pl.BlockSpec(kdec.shape, const3),
        pl.BlockSpec((1, W), lambda b, c: (0, 0)),
        pl.BlockSpec(w_s.shape, const3),
        pl.BlockSpec(sgb.shape, const3),
    ]
    args = [proj, cos, sin, dmask, qdec, kdec, sg_g.reshape(1, W), w_s, sgb]
    if has_s0:
        in_specs.append(pl.BlockSpec((None, n_heads, HEAD, HEAD), lambda b, c: (b, 0, 0, 0)))
        args.append(s0)
    out_specs = [
        pl.BlockSpec((rows_in, 2 * W), lambda b, c: (b * nc + c, 0)),
        pl.BlockSpec((None, n_heads, HEAD, HEAD), lambda b, c: (b, 0, 0, 0)),
    ]
    out_shape = [
        jax.ShapeDtypeStruct((B * L, 2 * W), out_dtype),
        jax.ShapeDtypeStruct((B, n_heads, HEAD, HEAD), F32),
    ]
    if emit_vb:
        out_specs.append(pl.BlockSpec((rows_in, W), lambda b, c: (b * nc + c, 0)))
        out_shape.append(jax.ShapeDtypeStruct((B * L, W), F32))
    body = functools.partial(_even_core_body, n_heads=n_heads, n_groups=n_groups, gch=gch,
                             rows_in=rows_in, has_s0=has_s0, emit_vb=emit_vb)
    return pl.pallas_call(
        body,
        grid=(B, nc),
        in_specs=in_specs,
        out_specs=out_specs,
        out_shape=out_shape,
        scratch_shapes=[pltpu.VMEM((n_heads, HEAD, HEAD), F32)],
        compiler_params=_cparams("parallel", "arbitrary"),
        name="even_core",
    )(*args)


def _mem_kv_body(m_ref, wk_ref, wv_ref, kg_ref, k_ref, v_ref, k16_ref, v16_ref, *, n_heads):
    m = m_ref[...].astype(BF16)
    k = jnp.dot(m, wk_ref[...], preferred_element_type=F32)
    kg = kg_ref[...]
    v = jnp.dot(m, wv_ref[...], preferred_element_type=F32)
    for h in range(n_heads):
        sl = slice(h * HEAD, (h + 1) * HEAD)
        kn = _rms(k[:, sl]) * kg
        k_ref[:, h, :] = kn
        v_ref[:, h, :] = v[:, sl]
        k16_ref[h] = kn.astype(BF16)
        v16_ref[h] = v[:, sl].astype(BF16)


def mem_kv(mem, w_k, w_v, k_g):
    B, NM, D = mem.shape
    depth, _, WX = w_k.shape
    nh = WX // HEAD
    out = jax.ShapeDtypeStruct((depth, B, NM, nh, HEAD), F32)
    out16 = jax.ShapeDtypeStruct((depth, B, nh, NM, HEAD), BF16)
    return pl.pallas_call(
        functools.partial(_mem_kv_body, n_heads=WX // HEAD),
        grid=(depth, B),
        in_specs=[
            pl.BlockSpec((None, NM, D), lambda l, b: (b, 0, 0)),
            pl.BlockSpec((None, D, WX), lambda l, b: (l, 0, 0)),
            pl.BlockSpec((None, D, WX), lambda l, b: (l, 0, 0)),
            pl.BlockSpec((None, 1, HEAD), lambda l, b: (l, 0, 0)),
        ],
        out_specs=[pl.BlockSpec((None, None, NM, nh, HEAD), lambda l, b: (l, b, 0, 0, 0))] * 2
        + [pl.BlockSpec((None, None, nh, NM, HEAD), lambda l, b: (l, b, 0, 0, 0))] * 2,
        out_shape=[out, out, out16, out16],
        compiler_params=_cparams("parallel", "parallel"),
        name="mem_kv",
    )(mem, w_k, w_v, k_g.reshape(depth, 1, HEAD))


def _mem_attention(q, qg, mk_ref, mv_ref, cd):
    outs = []
    for h in range(mk_ref.shape[0]):
        qn = (_rms(q[:, h * HEAD:(h + 1) * HEAD]) * qg).astype(cd)
        s = _mm(qn, mk_ref[h], NT)
        e = jnp.exp2(s - jnp.max(s, axis=-1, keepdims=True))
        outs.append(_mm(e.astype(cd), mv_ref[h]) * (1.0 / jnp.sum(e, axis=-1, keepdims=True)))
    return jnp.concatenate(outs, axis=-1).astype(cd)


def _cross_attn_body(a_ref, wa_ref, x_ref, g_ref, wq_ref, qg_ref, mk_ref, mv_ref, wo_ref, o_ref):
    cd = wq_ref.dtype
    x = x_ref[...] + _mm(a_ref[...], wa_ref[...])
    q = _mm((_rms(x) * g_ref[...]).astype(cd), wq_ref[...])
    o_ref[...] = x + _mm(_mem_attention(q, qg_ref[...], mk_ref, mv_ref, cd), wo_ref[...])


def _mem_attn_body(q_ref, qg_ref, mk_ref, mv_ref, o_ref):
    o_ref[...] = _mem_attention(q_ref[...], qg_ref[...], mk_ref, mv_ref, o_ref.dtype)


def mem_attn(q, qg, mk_all, mv_all, layer):
    B, L, WX = q.shape
    nh, NM = mk_all.shape[2], mk_all.shape[3]
    mem_spec = pl.BlockSpec((None, None, nh, NM, HEAD), lambda b: (layer, b, 0, 0, 0))
    return pl.pallas_call(
        _mem_attn_body,
        grid=(B,),
        in_specs=[
            pl.BlockSpec((None, L, WX), lambda b: (b, 0, 0)),
            pl.BlockSpec((1, HEAD), lambda b: (0, 0)),
            mem_spec,
            mem_spec,
        ],
        out_specs=pl.BlockSpec((None, L, WX), lambda b: (b, 0, 0)),
        out_shape=jax.ShapeDtypeStruct((B, L, WX), mk_all.dtype),
        compiler_params=_cparams("parallel"),
        name="mem_attn",
    )(q, (qg * (LOG2E * HEAD ** -0.5)).reshape(1, HEAD), mk_all, mv_all)


def cross_attn(a, wa, x, g, wq, qg, mk_all, mv_all, layer, wo, tm):
    B, L, D = x.shape
    Ka = a.shape[2]
    nh, NM = mk_all.shape[2], mk_all.shape[3]
    WX = nh * HEAD
    qg = qg * (LOG2E * HEAD ** -0.5)
    mem_spec = pl.BlockSpec((None, None, nh, NM, HEAD), lambda b, i: (layer, b, 0, 0, 0))
    return pl.pallas_call(
        _cross_attn_body,
        grid=(B, L // tm),
        in_specs=[
            pl.BlockSpec((None, tm, Ka), lambda b, i: (b, i, 0)),
            pl.BlockSpec((Ka, D), lambda b, i: (0, 0)),
            pl.BlockSpec((None, tm, D), lambda b, i: (b, i, 0)),
            pl.BlockSpec((1, D), lambda b, i: (0, 0)),
            pl.BlockSpec((D, WX), lambda b, i: (0, 0)),
            pl.BlockSpec((1, HEAD), lambda b, i: (0, 0)),
            mem_spec,
            mem_spec,
            pl.BlockSpec((WX, D), lambda b, i: (0, 0)),
        ],
        out_specs=pl.BlockSpec((None, tm, D), lambda b, i: (b, i, 0)),
        out_shape=jax.ShapeDtypeStruct((B, L, D), F32),
        compiler_params=_cparams("parallel", "parallel"),
        name="cross_attn",
    )(a, wa, x, g.reshape(1, D), wq, qg.reshape(1, HEAD), mk_all, mv_all, wo)


def _ffn_body(x_ref, g_ref, wg_ref, wu_ref, wd_ref, o_ref, h_ref, acc_ref):
    f = pl.program_id(1)

    @pl.when(f == 0)
    def _():
        h_ref[...] = (_rms(x_ref[...]) * g_ref[...]).astype(h_ref.dtype)
        acc_ref[...] = jnp.zeros_like(acc_ref)

    h = h_ref[...]
    a = _silu(_mm(h, wg_ref[...])) * _mm(h, wu_ref[...])
    acc_ref[...] += _mm(a.astype(h.dtype), wd_ref[...])

    @pl.when(f == pl.num_programs(1) - 1)
    def _():
        o_ref[...] = x_ref[...] + acc_ref[...]


def ffn(x, g, w_gu, w_down, tm, tf):
    M, D = x.shape
    F = w_down.shape[0]
    nf = F // tf
    return pl.pallas_call(
        _ffn_body,
        grid=(M // tm, nf),
        in_specs=[
            pl.BlockSpec((tm, D), lambda i, f: (i, 0)),
            pl.BlockSpec((1, D), lambda i, f: (0, 0)),
            pl.BlockSpec((D, tf), lambda i, f: (0, f)),
            pl.BlockSpec((D, tf), lambda i, f: (0, nf + f)),
            pl.BlockSpec((tf, D), lambda i, f: (f, 0)),
        ],
        out_specs=pl.BlockSpec((tm, D), lambda i, f: (i, 0)),
        out_shape=jax.ShapeDtypeStruct((M, D), F32),
        scratch_shapes=[pltpu.VMEM((tm, D), w_gu.dtype), pltpu.VMEM((tm, D), F32)],
        compiler_params=_cparams("parallel", "arbitrary"),
        name="ffn",
    )(x, g.reshape(1, D), w_gu, w_gu, w_down)


def _top2_gates(logits, n_experts):
    lane = lax.broadcasted_iota(jnp.int32, logits.shape, 1).astype(F32)
    neg = jnp.float32(-jnp.inf)
    logits = jnp.where(lane < n_experts, logits, neg)
    m1 = jnp.max(logits, axis=-1, keepdims=True)
    i1 = jnp.min(jnp.where(logits == m1, lane, float(LANES)), axis=-1, keepdims=True)
    rest = jnp.where(lane == i1, neg, logits)
    m2 = jnp.max(rest, axis=-1, keepdims=True)
    i2 = jnp.min(jnp.where(rest == m2, lane, float(LANES)), axis=-1, keepdims=True)
    e2 = jnp.exp(m2 - m1)
    den = 1.0 + e2
    return jnp.where(lane == i1, 1.0 / den, 0.0) + jnp.where(lane == i2, e2 / den, 0.0)


def _moe_body(x_ref, g_ref, wr_ref, wg_ref, wu_ref, wd_ref, o_ref, h_ref, acc_ref, gate_ref, *, n_experts):
    e = pl.program_id(1)

    @pl.when(e == 0)
    def _():
        h = _rms(x_ref[...]) * g_ref[...]
        h_ref[...] = h.astype(BF16)
        acc_ref[...] = jnp.zeros_like(acc_ref)
        gate_ref[...] = _top2_gates(_mm(h, wr_ref[...]), n_experts)

    h = h_ref[...]
    a = _silu(jnp.dot(h, wg_ref[...], preferred_element_type=F32)) * jnp.dot(h, wu_ref[...], preferred_element_type=F32)
    gates = gate_ref[...]
    lane = lax.broadcasted_iota(jnp.int32, gates.shape, 1)
    gate = jnp.sum(jnp.where(lane == e, gates, 0.0), axis=-1, keepdims=True)
    acc_ref[...] += gate * jnp.dot(a.astype(BF16), wd_ref[...], preferred_element_type=F32)

    @pl.when(e == pl.num_programs(1) - 1)
    def _():
        o_ref[...] = x_ref[...] + acc_ref[...]


def moe_dense(x, g, w_router, w_gu, w_down, tm):
    M, D = x.shape
    E, F, _ = w_down.shape
    w_router = jnp.pad(w_router, ((0, 0), (0, LANES - E)))
    return pl.pallas_call(
        functools.partial(_moe_body, n_experts=E),
        grid=(M // tm, E),
        in_specs=[
            pl.BlockSpec((tm, D), lambda i, e: (i, 0)),
            pl.BlockSpec((1, D), lambda i, e: (0, 0)),
            pl.BlockSpec((D, LANES), lambda i, e: (0, 0)),
            pl.BlockSpec((None, D, F), lambda i, e: (e, 0, 0)),
            pl.BlockSpec((None, D, F), lambda i, e: (e, 0, 1)),
            pl.BlockSpec((None, F, D), lambda i, e: (e, 0, 0)),
        ],
        out_specs=pl.BlockSpec((tm, D), lambda i, e: (i, 0)),
        out_shape=jax.ShapeDtypeStruct((M, D), F32),
        scratch_shapes=[pltpu.VMEM((tm, D), BF16), pltpu.VMEM((tm, D), F32), pltpu.VMEM((tm, LANES), F32)],
        compiler_params=_cparams("parallel", "arbitrary"),
        name="moe",
    )(x, g.reshape(1, D), w_router, w_gu, w_gu, w_down)


def _log2_terms(z2):
    sign_bit = jnp.int32(-2 ** 31)
    neg_abs = lax.bitcast_convert_type(lax.bitcast_convert_type(z2, jnp.int32) | sign_bit, F32)
    nlom = jnp.maximum(z2, 0.0) + jnp.log2(1.0 + jnp.exp2(neg_abs))
    return nlom, z2 - nlom


MASKED_LOG2 = -1e30


def _sb_prompt_body(q_ref, k_ref, v_ref, b_ref, u_ref, o_ref, acc_ref, r_ref, lom_ref, lb_ref, *, tk, rq):
    i = pl.program_id(2)
    tq = rq * tk
    n = rq * (i + 1)
    bias = b_ref[...]
    acc_ref[...] = jnp.zeros_like(acc_ref)
    r_ref[...] = jnp.zeros_like(r_ref)

    def keys(ref, kb):
        return ref[pl.ds(pl.multiple_of(kb * tk, tk), tk), :]

    def stage_a(kb, slot, masked):
        z = lax.dot_general(q_ref[...], keys(k_ref, kb), NT, preferred_element_type=F32) + bias
        nlom, lbeta = _log2_terms(z)
        if masked:
            row = lax.broadcasted_iota(jnp.int32, (tq, tk), 0)
            col = lax.broadcasted_iota(jnp.int32, (tq, tk), 1)
            keep = col + kb * tk < row + i * tq
            nlom = jnp.where(keep, nlom, 0.0)
            lbeta = jnp.where(keep, lbeta, MASKED_LOG2)
        lom_ref[slot] = nlom.astype(BF16)
        lb_ref[slot] = lbeta

    def stage_b(kb, slot):
        c = jnp.dot(lom_ref[slot], u_ref[...], preferred_element_type=F32)
        r = r_ref[...]
        a = jnp.exp2(lb_ref[slot] - c[:, :tk] - jnp.concatenate([r] * (tk // LANES), axis=1))
        acc_ref[...] += jnp.dot(a.astype(BF16), keys(v_ref, kb), preferred_element_type=F32)
        r_ref[...] = r + c[:, tk:]

    stage_a(n - 1, 0, True)
    for t in range(1, rq):
        stage_a(n - 1 - t, t % 2, True)
        stage_b(n - t, (t - 1) % 2)

    def body(p, carry):
        kb = n - 1 - rq - 2 * p
        stage_a(kb, 0, False)
        stage_b(kb + 1, 1)
        stage_a(kb - 1, 1, False)
        stage_b(kb, 0)
        return carry

    lax.fori_loop(0, (n - rq) // 2, body, 0)
    stage_b(0, 1)
    o_ref[...] = acc_ref[...].astype(o_ref.dtype)


def _suffix_matrix(t):
    j = np.arange(t)[:, None]
    s = np.arange(t)[None, :]
    u = np.concatenate([(j > s).astype(np.float32), np.ones((t, LANES), np.float32)], axis=1)
    return jnp.asarray(u, dtype=BF16)


def sb_prompt(qkv, bias, B, S, tk, rq):
    W = qkv.shape[2]
    H = W // HEAD
    assert rq % 2 == 0
    tq = rq * tk
    nq = S // tq
    bias_b = jnp.broadcast_to((bias.astype(F32) * LOG2E)[:, None, None], (H, 1, tk))
    return pl.pallas_call(
        functools.partial(_sb_prompt_body, tk=tk, rq=rq),
        grid=(B, H, nq),
        in_specs=[
            pl.BlockSpec((None, tq, HEAD), lambda b, h, i: (0, b * nq + i, h)),
            pl.BlockSpec((None, S, HEAD), lambda b, h, i: (1, b, h)),
            pl.BlockSpec((None, S, HEAD), lambda b, h, i: (2, b, h)),
            pl.BlockSpec((None, 1, tk), lambda b, h, i: (h, 0, 0)),
            pl.BlockSpec((tk, tk + LANES), lambda b, h, i: (0, 0)),
        ],
        out_specs=pl.BlockSpec((tq, HEAD), lambda b, h, i: (b * nq + i, h)),
        out_shape=jax.ShapeDtypeStruct((B * S, W), BF16),
        scratch_shapes=[pltpu.VMEM((tq, HEAD), F32), pltpu.VMEM((tq, LANES), F32),
                        pltpu.VMEM((2, tq, tk), BF16), pltpu.VMEM((2, tq, tk), F32)],
        compiler_params=_cparams("parallel", "parallel", "arbitrary"),
        name="sb_prompt",
    )(qkv, qkv, qkv, bias_b, _suffix_matrix(tk))


def _sb_sample_body(pt_ref, q_ref, kn_ref, vn_ref, b_ref, u_ref, *rest, n_heads, ppb, t_valid):
    kp_refs = rest[:ppb]
    vp_refs = rest[ppb:2 * ppb]
    o_ref, qx_ref, acc_ref, r_ref, nl_ref, lb_ref, cs_ref = rest[2 * ppb:]
    del pt_ref
    pg = pl.program_id(1)
    G = LANES // n_heads
    bias = b_ref[...]

    def stage_a(get_k, npages, masked):
        z = bias
        for h in range(n_heads):
            k_h = jnp.concatenate([get_k(n, h) for n in range(npages)], axis=0)
            z = z + lax.dot_general(k_h, qx_ref[h], NT, preferred_element_type=F32)
        nlom, lbeta = _log2_terms(z)
        if masked:
            col = lax.broadcasted_iota(jnp.int32, z.shape, 1)
            row = lax.broadcasted_iota(jnp.int32, z.shape, 0)
            t = col & (G - 1)
            keep = (row < t) & (t < t_valid)
            nlom = jnp.where(keep, nlom, 0.0)
            lbeta = jnp.where(keep, lbeta, MASKED_LOG2)
        nlom_p = [nlom[n * PAGE:(n + 1) * PAGE] for n in range(npages)]
        sums = jnp.concatenate([jnp.sum(x, axis=0, keepdims=True) for x in nlom_p], axis=0)
        return jnp.concatenate(nlom_p, axis=1).astype(BF16), lbeta, sums

    def stage_b(nl, lbeta, sums, get_v, npages):
        after = jnp.dot(u_ref[...], nl, preferred_element_type=F32)
        r = r_ref[...]
        a_t = []
        for n in range(npages):
            a = jnp.exp2(lbeta[n * PAGE:(n + 1) * PAGE] - after[:, n * LANES:(n + 1) * LANES] - r)
            a_t.append(a.T.astype(BF16))
            r = r + sums[n:n + 1]
        r_ref[...] = r
        a_t = jnp.concatenate(a_t, axis=1)
        for h in range(n_heads):
            v_h = jnp.concatenate([get_v(n, h) for n in range(npages)], axis=0)
            acc_ref[h] += jnp.dot(a_t[h * G:(h + 1) * G], v_h, preferred_element_type=F32)

    def pad_rows(x, before, total):
        parts = []
        if before:
            parts.append(jnp.zeros((before, x.shape[1]), x.dtype))
        parts.append(x)
        if total - before - x.shape[0]:
            parts.append(jnp.zeros((total - before - x.shape[0], x.shape[1]), x.dtype))
        return jnp.concatenate(parts, axis=0)

    @pl.when(pg == 0)
    def _():
        for h in range(n_heads):
            sl = slice(h * HEAD, (h + 1) * HEAD)
            qx_ref[h] = pad_rows(q_ref[:, sl] * (LOG2E * HEAD ** -0.5), h * G, LANES).astype(BF16)
        acc_ref[...] = jnp.zeros_like(acc_ref)
        r_ref[...] = jnp.zeros_like(r_ref)

        def new_rows(ref):
            return lambda n, h: pad_rows(ref[:, h * HEAD:(h + 1) * HEAD], 0, PAGE).astype(BF16)

        stage_b(*stage_a(new_rows(kn_ref), 1, True), new_rows(vn_ref), 1)
        nl_ref[1] = jnp.zeros(nl_ref.shape[1:], nl_ref.dtype)
        lb_ref[1] = jnp.full(lb_ref.shape[1:], MASKED_LOG2, lb_ref.dtype)
        cs_ref[1] = jnp.zeros(cs_ref.shape[1:], cs_ref.dtype)

    def pool_rows(refs):
        return lambda n, h: refs[n][pl.ds(h, PAGE, stride=n_heads), :].astype(BF16)

    done = lax.rem(pg + 1, 2)
    stage_b(nl_ref[done], lb_ref[done], cs_ref[done], pool_rows(vp_refs), ppb)
    cur = lax.rem(pg, 2)
    nl, lbeta, sums = stage_a(pool_rows(kp_refs), ppb, False)
    nl_ref[cur] = nl
    lb_ref[cur] = lbeta
    cs_ref[cur] = sums

    @pl.when(pg == pl.num_programs(1) - 1)
    def _():
        for h in range(n_heads):
            o_ref[:, h * HEAD:(h + 1) * HEAD] = acc_ref[h][:T_PAD].astype(o_ref.dtype)


def sb_sample(proj, bias, pool_k, pool_v, layer, page_table, t_valid, ppb, out_dtype):
    W = proj.shape[2]
    H = W // HEAD
    G = LANES // H
    B, n_pages = page_table.shape
    npg = n_pages // ppb
    bias_b = jnp.repeat(bias.astype(F32) * LOG2E, G).reshape(1, LANES)
    j = np.arange(PAGE)
    u_t = jnp.asarray((j[None, :] > j[:, None]).astype(np.float32), dtype=BF16)

    def rows_view(pool):
        return pool.reshape(pool.shape[0], pool.shape[1], PAGE * H, HEAD)

    def tok(g):
        return pl.BlockSpec((None, T_PAD, W), lambda b, p, pt: (g, b, 0))

    def pool_spec(n, lag):
        def index(b, p, pt):
            grp = jnp.clip(p - lag, 0, npg - 1)
            return layer, pt[b, n_pages - 1 - (grp * ppb + n)], 0, 0

        return pl.BlockSpec((None, None, PAGE * H, HEAD), index)

    grid_spec = pltpu.PrefetchScalarGridSpec(
        num_scalar_prefetch=1,
        grid=(B, npg + 1),
        in_specs=[tok(0), tok(1), tok(2),
                  pl.BlockSpec((1, LANES), lambda b, p, pt: (0, 0)),
                  pl.BlockSpec((PAGE, PAGE), lambda b, p, pt: (0, 0))]
        + [pool_spec(n, 0) for n in range(ppb)] + [pool_spec(n, 1) for n in range(ppb)],
        out_specs=pl.BlockSpec((T_PAD, W), lambda b, p, pt: (b, 0)),
        scratch_shapes=[pltpu.VMEM((H, LANES, HEAD), BF16), pltpu.VMEM((H, G, HEAD), F32),
                        pltpu.VMEM((1, LANES), F32),
                        pltpu.VMEM((2, PAGE, ppb * LANES), BF16), pltpu.VMEM((2, ppb * PAGE, LANES), F32),
                        pltpu.VMEM((2, ppb, LANES), F32)],
    )
    return pl.pallas_call(
        functools.partial(_sb_sample_body, n_heads=H, ppb=ppb, t_valid=t_valid),
        grid_spec=grid_spec,
        out_shape=jax.ShapeDtypeStruct((B * T_PAD, W), out_dtype),
        compiler_params=_cparams("parallel", "arbitrary"),
        name="sb_sample",
    )(page_table, proj, proj, proj, bias_b, u_t, *([rows_view(pool_k)] * ppb), *([rows_view(pool_v)] * ppb))


def _rope_tables(pos):
    half = HEAD // 2
    inv = 1.0 / (ROPE_BASE ** jnp.linspace(0.0, 1.0, half, dtype=F32))
    ang = pos[:, None] * inv[None, :]
    cos, sin = jnp.cos(ang), jnp.sin(ang)
    return jnp.concatenate([cos, cos], axis=-1), jnp.concatenate([-sin, sin], axis=-1)


def _retention_tables(n_heads, chunk):
    log_g = np.log(1.0 - 2.0 ** (-5.0 - np.arange(n_heads, dtype=np.float64)))
    idx = np.arange(HEAD, dtype=np.float64)
    diff = idx[:, None] - idx[None, :]
    dmask = np.where(diff >= 0, np.exp(log_g[:, None, None] * np.maximum(diff, 0.0)), 0.0)
    qdec = np.exp(log_g[:, None] * (idx + 1.0)[None, :])
    kdec = np.where(idx[None, :] < chunk, np.exp(log_g[:, None] * (chunk - 1.0 - idx)[None, :]), 0.0)
    bc = lambda t: jnp.asarray(np.broadcast_to(t[:, :, None], (n_heads, HEAD, HEAD)), dtype=F32)
    gch = tuple(float(x) for x in np.exp(log_g * chunk))
    return jnp.asarray(dmask, dtype=F32), bc(qdec), bc(kdec), gch


def kernel(x_prompt, x_sample, mem_prompt, state_ret, cache_sb_k, cache_sb_v, page_table, cache_mem_k, cache_mem_v, norm_mix_g, norm_x_g, norm_ffn_g, w_in_even, w_out_even, sg_norm_g, sg_w_s, sg_b, w_qkv_sb, w_o_sb, sb_bias, w_q_x, w_k_x, w_v_x, w_o_x, q_norm_x_g, k_norm_x_g, w_gu_dense, w_down_dense, w_router, w_gu_moe, w_down_moe):
    B, S, D = x_prompt.shape
    BS, T, _ = x_sample.shape
    depth = norm_mix_g.shape[0]
    n_heads_ret = state_ret.shape[2]
    n_experts = w_router.shape[2]
    past = page_table.shape[1] * PAGE
    H_sb = cache_sb_k.shape[3]
    W_sb = H_sb * HEAD
    W_x = w_q_x.shape[2]

    bf = lambda w: w.astype(BF16)
    TM = min(1024, B * S)
    MS = BS * T_PAD

    xp = x_prompt.reshape(B * S, D)
    xs = jnp.pad(x_sample, ((0, 0), (0, T_PAD - T), (0, 0))).reshape(MS, D)

    cos_p, sin_p = _rope_tables(jnp.arange(S, dtype=F32))
    cos_s, sin_s = _rope_tables(past + jnp.arange(HEAD, dtype=F32))
    dmask, qdec, kdec_p, gch_p = _retention_tables(n_heads_ret, HEAD)
    _, _, kdec_s, gch_s = _retention_tables(n_heads_ret, T)

    mk_p, mv_p, mk_p16, mv_p16 = mem_kv(mem_prompt, bf(w_k_x), bf(w_v_x), k_norm_x_g)
    n_f32 = max(depth - 2, 0)
    head_major = lambda c: jnp.transpose(c, (0, 1, 3, 2, 4))
    cmk, cmv = head_major(cache_mem_k[:n_f32]), head_major(cache_mem_v[:n_f32])
    cmk16, cmv16 = bf(head_major(cache_mem_k[n_f32:])), bf(head_major(cache_mem_v[n_f32:]))

    ret_p, ret_s, sgv_s = [], [], []
    kv_p = None
    sbk_s, sbv_s = [], []
    for l in range(depth):
        i = l // 2
        sample_f32 = l + 2 < depth
        if l % 2 == 0:
            w_in, w_out = bf(w_in_even[i]), bf(w_out_even[i])
            sgb = jnp.broadcast_to(sg_b[i][:, :, None], sg_b[i].shape + (HEAD,))
            proj = norm_mm(xp, norm_mix_g[l], w_in, 6, TM, 512)
            y, sp = even_core(proj, B, S, HEAD, cos_p, sin_p, dmask, qdec, kdec_p, gch_p,
                              sg_norm_g[i], sg_w_s[i], sgb, None, False, BF16)
            mix_p = (y, w_out)
            if sample_f32:
                w_in, w_out = w_in_even[i], w_out_even[i]
            proj = norm_mm(xs, norm_mix_g[l], w_in, 6, MS, 512)
            y, ss, vs = even_core(proj, BS, T_PAD, T_PAD, cos_s, sin_s, dmask, qdec, kdec_s, gch_s,
                                  sg_norm_g[i], sg_w_s[i], sgb, state_ret[i], True, w_in.dtype)
            xs = mm_res(y, w_out, xs, MS, 512)
            ret_p.append(sp)
            ret_s.append(ss)
            sgv_s.append(vs.reshape(BS, T_PAD, -1)[:, :T])
        else:
            w_qkv, w_o = bf(w_qkv_sb[i]), bf(w_o_sb[i])
            qkv16, *kv_p = norm_qkv(xp, norm_mix_g[l], w_qkv, LOG2E * HEAD ** -0.5, i, depth // 2, kv_p,
                                    TM)
            o = sb_prompt(qkv16, sb_bias[i], B, S, 256, 2)
            mix_p = (o, w_o)
            if sample_f32:
                w_qkv, w_o = w_qkv_sb[i], w_o_sb[i]
            proj = norm_mm(xs, norm_mix_g[l], w_qkv, 3, MS, 512)
            o = sb_sample(proj, sb_bias[i], cache_sb_k, cache_sb_v, i, page_table, T, math.gcd(8, page_table.shape[1]), w_o.dtype)
            xs = mm_res(o, w_o, xs, MS, 512)
            sbk_s.append(proj[1].reshape(BS, T_PAD, H_sb, HEAD)[:, :T])
            sbv_s.append(proj[2].reshape(BS, T_PAD, H_sb, HEAD)[:, :T])
        wq, wo = bf(w_q_x[l]), bf(w_o_x[l])
        xp = cross_attn(mix_p[0].reshape(B, S, -1), mix_p[1], xp.reshape(B, S, D), norm_x_g[l], wq, q_norm_x_g[l],
                        mk_p16, mv_p16, l, wo, 512).reshape(B * S, D)
        mem_s = (cmk16, cmv16, l - n_f32)
        if sample_f32:
            wq, wo, mem_s = w_q_x[l], w_o_x[l], (cmk, cmv, l)
        q_s = norm_mm(xs, norm_x_g[l], wq, 1, MS, 512)
        o_s = mem_attn(q_s.reshape(BS, T_PAD, -1), q_norm_x_g[l], *mem_s)
        xs = mm_res(o_s.reshape(MS, -1), wo, xs, MS, 512)
        if l % 2 == 0:
            w_gu, w_down = bf(w_gu_dense[i]), bf(w_down_dense[i])
            xp = ffn(xp, norm_ffn_g[l], w_gu, w_down, TM, 256)
            if sample_f32:
                w_gu, w_down = w_gu_dense[i], w_down_dense[i]
            xs = ffn(xs, norm_ffn_g[l], w_gu, w_down, MS, 256)
        else:
            w_gu, w_down = bf(w_gu_moe[i]), bf(w_down_moe[i])
            xp = moe_dense(xp, norm_ffn_g[l], w_router[i], w_gu, w_down, min(512, B * S))
            xs = moe_dense(xs, norm_ffn_g[l], w_router[i], w_gu, w_down, MS)

    y_p = xp.reshape(B, S, D)
    y_s = xs.reshape(BS, T_PAD, D)[:, :T]
    return (y_p, y_s, jnp.stack(ret_p), jnp.stack(ret_s), jnp.stack(sgv_s),
            kv_p[0].reshape(-1, B, S, H_sb, HEAD), kv_p[1].reshape(-1, B, S, H_sb, HEAD),
            jnp.stack(sbk_s), jnp.stack(sbv_s), mk_p, mv_p)
```

```python
import functools
import math

import numpy as np
import jax
import jax.numpy as jnp
from jax import lax
from jax.experimental import pallas as pl
from jax.experimental.pallas import tpu as pltpu

F32 = jnp.float32
BF16 = jnp.bfloat16
EPS = 1e-6
ROPE_BASE = 10000.0
LOG2E = math.log2(math.e)
LANES = 128
VMEM_LIMIT_BYTES = 56 * 1024 * 1024
HEAD = 128
PAGE = 128
T_PAD = 8
NN = (((1,), (0,)), ((), ()))
NT = (((1,), (1,)), ((), ()))
TN = (((0,), (0,)), ((), ()))


def _cparams(*sem):
    return pltpu.CompilerParams(dimension_semantics=sem, vmem_limit_bytes=VMEM_LIMIT_BYTES)


def _rms(x):
    return x * lax.rsqrt(jnp.mean(x * x, axis=-1, keepdims=True) + EPS)


def _gelu(x):
    return 0.5 * x * (1.0 + lax.erf(x * (2.0 ** -0.5)))


def _silu(x):
    return x * jax.nn.sigmoid(x)


def _mm(a, b, dims=NN):
    assert a.dtype == b.dtype
    precision = lax.Precision.HIGHEST if a.dtype == F32 else None
    return lax.dot_general(a, b, dims, precision=precision, preferred_element_type=F32)


def _norm_mm_body(x_ref, g_ref, w_ref, o_ref, h_ref):
    @pl.when(pl.program_id(1) == 0)
    def _():
        h_ref[...] = (_rms(x_ref[...]) * g_ref[...]).astype(h_ref.dtype)

    o_ref[...] = _mm(h_ref[...], w_ref[...])


def norm_mm(x, g, w, groups, tm, tn):
    M, K = x.shape
    N = w.shape[1]
    ng = N // groups
    npg = ng // tn
    return pl.pallas_call(
        _norm_mm_body,
        grid=(M // tm, N // tn),
        in_specs=[
            pl.BlockSpec((tm, K), lambda i, j: (i, 0)),
            pl.BlockSpec((1, K), lambda i, j: (0, 0)),
            pl.BlockSpec((K, tn), lambda i, j: (0, j)),
        ],
        out_specs=pl.BlockSpec((None, tm, tn), lambda i, j: (j // npg, i, j % npg)),
        out_shape=jax.ShapeDtypeStruct((groups, M, ng), F32),
        scratch_shapes=[pltpu.VMEM((tm, K), w.dtype)],
        compiler_params=_cparams("parallel", "arbitrary"),
        name="norm_mm",
    )(x, g.reshape(1, K), w)


def _norm_qkv_body(x_ref, g_ref, w_ref, *rest, q_scale, n_heads, n_alias):
    qkv_ref, k_ref, v_ref, h_ref = rest[n_alias:]
    grp = pl.program_id(1)

    @pl.when(grp == 0)
    def _():
        h_ref[...] = (_rms(x_ref[...]) * g_ref[...]).astype(h_ref.dtype)

    o = _mm(h_ref[...], w_ref[...])
    qkv_ref[...] = (o * jnp.where(grp == 0, q_scale, 1.0)).astype(qkv_ref.dtype)

    def heads_out(ref):
        for h in range(n_heads):
            ref[:, h, :] = o[:, h * HEAD:(h + 1) * HEAD]

    pl.when(grp == 1)(lambda: heads_out(k_ref))
    pl.when(grp == 2)(lambda: heads_out(v_ref))


def norm_qkv(x, g, w, q_scale, layer_slot, n_slots, kv_prev, tm):
    M, K = x.shape
    W = w.shape[1] // 3
    nh = W // HEAD
    kv_shape = jax.ShapeDtypeStruct((n_slots, M, nh, HEAD), F32)
    kv_spec = pl.BlockSpec((None, tm, nh, HEAD), lambda i, j: (layer_slot, i, 0, 0))
    n_alias = 0 if kv_prev is None else 2
    return pl.pallas_call(
        functools.partial(_norm_qkv_body, q_scale=q_scale, n_heads=nh, n_alias=n_alias),
        grid=(M // tm, 3),
        in_specs=[
            pl.BlockSpec((tm, K), lambda i, j: (i, 0)),
            pl.BlockSpec((1, K), lambda i, j: (0, 0)),
            pl.BlockSpec((K, W), lambda i, j: (0, j)),
        ] + [pl.BlockSpec(memory_space=pl.ANY)] * n_alias,
        out_specs=[pl.BlockSpec((None, tm, W), lambda i, j: (j, i, 0)), kv_spec, kv_spec],
        out_shape=[jax.ShapeDtypeStruct((3, M, W), BF16), kv_shape, kv_shape],
        input_output_aliases={3: 1, 4: 2} if n_alias else {},
        scratch_shapes=[pltpu.VMEM((tm, K), w.dtype)],
        compiler_params=_cparams("parallel", "arbitrary"),
        name="norm_qkv",
    )(x, g.reshape(1, K), w, *(kv_prev or ()))


def _mm_res_body(a_ref, w_ref, r_ref, o_ref):
    o_ref[...] = r_ref[...] + _mm(a_ref[...], w_ref[...])


def mm_res(a, w, res, tm, tn):
    M, K = a.shape
    N = w.shape[1]
    tn = min(tn, N)
    return pl.pallas_call(
        _mm_res_body,
        grid=(M // tm, N // tn),
        in_specs=[
            pl.BlockSpec((tm, K), lambda i, j: (i, 0)),
            pl.BlockSpec((K, tn), lambda i, j: (0, j)),
            pl.BlockSpec((tm, tn), lambda i, j: (i, j)),
        ],
        out_specs=pl.BlockSpec((tm, tn), lambda i, j: (i, j)),
        out_shape=jax.ShapeDtypeStruct((M, N), F32),
        compiler_params=_cparams("parallel", "arbitrary"),
        name="mm_res",
    )(a, w, res)


def _even_core_body(*refs, n_heads, n_groups, gch, rows_in, has_s0, emit_vb):
    it = iter(refs)
    p_ref, cos_ref, sin_ref, dmask_ref, qdec_ref, kdec_ref, sgg_ref, ws_ref, sgb_ref = (next(it) for _ in range(9))
    s0_ref = next(it) if has_s0 else None
    y_ref, sout_ref = next(it), next(it)
    vb_ref = next(it) if emit_vb else None
    s_ref = next(it)
    C = HEAD
    cd = y_ref.dtype
    c = pl.program_id(1)
    w_ret = n_heads * HEAD

    @pl.when(c == 0)
    def _():
        if has_s0:
            s_ref[...] = s0_ref[...]
        else:
            s_ref[...] = jnp.zeros_like(s_ref)

    def rows(x):
        if rows_in == C:
            return x
        return jnp.concatenate([x, jnp.zeros((C - rows_in, x.shape[1]), x.dtype)], axis=0)

    cos = cos_ref[...]
    sin = sin_ref[...]

    def rot(x):
        return x * cos + pltpu.roll(x, HEAD // 2, 1) * sin

    for h in range(n_heads):
        sl = slice(h * HEAD, (h + 1) * HEAD)
        q = rot(rows(p_ref[0, :, sl]))
        k = rot(rows(p_ref[1, :, sl])) * (HEAD ** -0.5)
        v = rows(p_ref[2, :, sl]).astype(cd)
        g = rows(p_ref[3, :, sl])
        intra = _mm(q.astype(cd), k.astype(cd), NT) * dmask_ref[h]
        o = _mm(intra.astype(cd), v)
        s = s_ref[h]
        o = o + _mm((q * qdec_ref[h]).astype(cd), s.astype(cd))
        kv = _mm((k * kdec_ref[h]).astype(cd), v, TN)
        s_ref[h] = gch[h] * s + kv
        oa = _rms(o) * _silu(g)
        y_ref[:, sl] = oa[:rows_in].astype(y_ref.dtype)

    u = _gelu(rows(p_ref[4]))
    vg = _gelu(rows(p_ref[5]))
    xc = vg - jnp.mean(vg, axis=-1, keepdims=True)
    vb = xc * lax.rsqrt(jnp.mean(xc * xc, axis=-1, keepdims=True) + EPS) * sgg_ref[...]
    if emit_vb:
        vb_ref[...] = vb[:rows_in]
    row = lax.broadcasted_iota(jnp.int32, (C, C), 0)
    col = lax.broadcasted_iota(jnp.int32, (C, C), 1)
    for gi in range(n_groups):
        sl = slice(gi * HEAD, (gi + 1) * HEAD)
        w = jnp.where(col <= row, ws_ref[gi], 0.0).astype(cd)
        f = _mm(w, vb[:, sl].astype(cd)) + sgb_ref[gi]
        ob = u[:, sl] * f
        y_ref[:, w_ret + gi * HEAD:w_ret + (gi + 1) * HEAD] = ob[:rows_in].astype(y_ref.dtype)

    @pl.when(c == pl.num_programs(1) - 1)
    def _():
        sout_ref[...] = s_ref[...]


def even_core(proj, B, L, rows_in, cos, sin, dmask, qdec, kdec, gch, sg_g, w_s, sgb, s0, emit_vb, out_dtype):
    n_heads = dmask.shape[0]
    n_groups = w_s.shape[0]
    W = proj.shape[2]
    nc = L // rows_in
    has_s0 = s0 is not None
    const3 = lambda b, c: (0, 0, 0)
    in_specs = [
        pl.BlockSpec((6, rows_in, W), lambda b, c: (0, b * nc + c, 0)),
        pl.BlockSpec((HEAD, HEAD), lambda b, c: (c, 0)),
        pl.BlockSpec((HEAD, HEAD), lambda b, c: (c, 0)),
        pl.BlockSpec(dmask.shape, const3),
        pl.BlockSpec(qdec.shape, const3),
        pl.BlockSpec(kdec.shape, const3),
        pl.BlockSpec((1, W), lambda b, c: (0, 0)),
        pl.BlockSpec(w_s.shape, const3),
        pl.BlockSpec(sgb.shape, const3),
    ]
    args = [proj, cos, sin, dmask, qdec, kdec, sg_g.reshape(1, W), w_s, sgb]
    if has_s0:
        in_specs.append(pl.BlockSpec((None, n_heads, HEAD, HEAD), lambda b, c: (b, 0, 0, 0)))
        args.append(s0)
    out_specs = [
        pl.BlockSpec((rows_in, 2 * W), lambda b, c: (b * nc + c, 0)),
        pl.BlockSpec((None, n_heads, HEAD, HEAD), lambda b, c: (b, 0, 0, 0)),
    ]
    out_shape = [
        jax.ShapeDtypeStruct((B * L, 2 * W), out_dtype),
        jax.ShapeDtypeStruct((B, n_heads, HEAD, HEAD), F32),
    ]
    if emit_vb:
        out_specs.append(pl.BlockSpec((rows_in, W), lambda b, c: (b * nc + c, 0)))
        out_shape.append(jax.ShapeDtypeStruct((B * L, W), F32))
    body = functools.partial(_even_core_body, n_heads=n_heads, n_groups=n_groups, gch=gch,
                             rows_in=rows_in, has_s0=has_s0, emit_vb=emit_vb)
    return pl.pallas_call(
        body,
        grid=(B, nc),
        in_specs=in_specs,
        out_specs=out_specs,
        out_shape=out_shape,
        scratch_shapes=[pltpu.VMEM((n_heads, HEAD, HEAD), F32)],
        compiler_params=_cparams("parallel", "arbitrary"),
        name="even_core",
    )(*args)


def _mem_kv_body(m_ref, wk_ref, wv_ref, kg_ref, k_ref, v_ref, k16_ref, v16_ref, *, n_heads):
    m = m_ref[...].astype(BF16)
    k = jnp.dot(m, wk_ref[...], preferred_element_type=F32)
    kg = kg_ref[...]
    v = jnp.dot(m, wv_ref[...], preferred_element_type=F32)
    for h in range(n_heads):
        sl = slice(h * HEAD, (h + 1) * HEAD)
        kn = _rms(k[:, sl]) * kg
        k_ref[:, h, :] = kn
        v_ref[:, h, :] = v[:, sl]
        k16_ref[h] = kn.astype(BF16)
        v16_ref[h] = v[:, sl].astype(BF16)


def mem_kv(mem, w_k, w_v, k_g):
    B, NM, D = mem.shape
    depth, _, WX = w_k.shape
    nh = WX // HEAD
    out = jax.ShapeDtypeStruct((depth, B, NM, nh, HEAD), F32)
    out16 = jax.ShapeDtypeStruct((depth, B, nh, NM, HEAD), BF16)
    return pl.pallas_call(
        functools.partial(_mem_kv_body, n_heads=WX // HEAD),
        grid=(depth, B),
        in_specs=[
            pl.BlockSpec((None, NM, D), lambda l, b: (b, 0, 0)),
            pl.BlockSpec((None, D, WX), lambda l, b: (l, 0, 0)),
            pl.BlockSpec((None, D, WX), lambda l, b: (l, 0, 0)),
            pl.BlockSpec((None, 1, HEAD), lambda l, b: (l, 0, 0)),
        ],
        out_specs=[pl.BlockSpec((None, None, NM, nh, HEAD), lambda l, b: (l, b, 0, 0, 0))] * 2
        + [pl.BlockSpec((None, None, nh, NM, HEAD), lambda l, b: (l, b, 0, 0, 0))] * 2,
        out_shape=[out, out, out16, out16],
        compiler_params=_cparams("parallel", "parallel"),
        name="mem_kv",
    )(mem, w_k, w_v, k_g.reshape(depth, 1, HEAD))


def _mem_attention(q, qg, mk_ref, mv_ref, cd):
    outs = []
    for h in range(mk_ref.shape[0]):
        qn = (_rms(q[:, h * HEAD:(h + 1) * HEAD]) * qg).astype(cd)
        s = _mm(qn, mk_ref[h], NT)
        e = jnp.exp2(s - jnp.max(s, axis=-1, keepdims=True))
        outs.append(_mm(e.astype(cd), mv_ref[h]) * (1.0 / jnp.sum(e, axis=-1, keepdims=True)))
    return jnp.concatenate(outs, axis=-1).astype(cd)


def _cross_attn_body(a_ref, wa_ref, x_ref, g_ref, wq_ref, qg_ref, mk_ref, mv_ref, wo_ref, o_ref):
    cd = wq_ref.dtype
    x = x_ref[...] + _mm(a_ref[...], wa_ref[...])
    q = _mm((_rms(x) * g_ref[...]).astype(cd), wq_ref[...])
    o_ref[...] = x + _mm(_mem_attention(q, qg_ref[...], mk_ref, mv_ref, cd), wo_ref[...])


def _mem_attn_body(q_ref, qg_ref, mk_ref, mv_ref, o_ref):
    o_ref[...] = _mem_attention(q_ref[...], qg_ref[...], mk_ref, mv_ref, o_ref.dtype)


def mem_attn(q, qg, mk_all, mv_all, layer):
    B, L, WX = q.shape
    nh, NM = mk_all.shape[2], mk_all.shape[3]
    mem_spec = pl.BlockSpec((None, None, nh, NM, HEAD), lambda b: (layer, b, 0, 0, 0))
    return pl.pallas_call(
        _mem_attn_body,
        grid=(B,),
        in_specs=[
            pl.BlockSpec((None, L, WX), lambda b: (b, 0, 0)),
            pl.BlockSpec((1, HEAD), lambda b: (0, 0)),
            mem_spec,
            mem_spec,
        ],
        out_specs=pl.BlockSpec((None, L, WX), lambda b: (b, 0, 0)),
        out_shape=jax.ShapeDtypeStruct((B, L, WX), mk_all.dtype),
        compiler_params=_cparams("parallel"),
        name="mem_attn",
    )(q, (qg * (LOG2E * HEAD ** -0.5)).reshape(1, HEAD), mk_all, mv_all)


def cross_attn(a, wa, x, g, wq, qg, mk_all, mv_all, layer, wo, tm):
    B, L, D = x.shape
    Ka = a.shape[2]
    nh, NM = mk_all.shape[2], mk_all.shape[3]
    WX = nh * HEAD
    qg = qg * (LOG2E * HEAD ** -0.5)
    mem_spec = pl.BlockSpec((None, None, nh, NM, HEAD), lambda b, i: (layer, b, 0, 0, 0))
    return pl.pallas_call(
        _cross_attn_body,
        grid=(B, L // tm),
        in_specs=[
            pl.BlockSpec((None, tm, Ka), lambda b, i: (b, i, 0)),
            pl.BlockSpec((Ka, D), lambda b, i: (0, 0)),
            pl.BlockSpec((None, tm, D), lambda b, i: (b, i, 0)),
            pl.BlockSpec((1, D), lambda b, i: (0, 0)),
            pl.BlockSpec((D, WX), lambda b, i: (0, 0)),
            pl.BlockSpec((1, HEAD), lambda b, i: (0, 0)),
            mem_spec,
            mem_spec,
            pl.BlockSpec((WX, D), lambda b, i: (0, 0)),
        ],
        out_specs=pl.BlockSpec((None, tm, D), lambda b, i: (b, i, 0)),
        out_shape=jax.ShapeDtypeStruct((B, L, D), F32),
        compiler_params=_cparams("parallel", "parallel"),
        name="cross_attn",
    )(a, wa, x, g.reshape(1, D), wq, qg.reshape(1, HEAD), mk_all, mv_all, wo)


def _ffn_body(x_ref, g_ref, wg_ref, wu_ref, wd_ref, o_ref, h_ref, acc_ref):
    f = pl.program_id(1)

    @pl.when(f == 0)
    def _():
        h_ref[...] = (_rms(x_ref[...]) * g_ref[...]).astype(h_ref.dtype)
        acc_ref[...] = jnp.zeros_like(acc_ref)

    h = h_ref[...]
    a = _silu(_mm(h, wg_ref[...])) * _mm(h, wu_ref[...])
    acc_ref[...] += _mm(a.astype(h.dtype), wd_ref[...])

    @pl.when(f == pl.num_programs(1) - 1)
    def _():
        o_ref[...] = x_ref[...] + acc_ref[...]


def ffn(x, g, w_gu, w_down, tm, tf):
    M, D = x.shape
    F = w_down.shape[0]
    nf = F // tf
    return pl.pallas_call(
        _ffn_body,
        grid=(M // tm, nf),
        in_specs=[
            pl.BlockSpec((tm, D), lambda i, f: (i, 0)),
            pl.BlockSpec((1, D), lambda i, f: (0, 0)),
            pl.BlockSpec((D, tf), lambda i, f: (0, f)),
            pl.BlockSpec((D, tf), lambda i, f: (0, nf + f)),
            pl.BlockSpec((tf, D), lambda i, f: (f, 0)),
        ],
        out_specs=pl.BlockSpec((tm, D), lambda i, f: (i, 0)),
        out_shape=jax.ShapeDtypeStruct((M, D), F32),
        scratch_shapes=[pltpu.VMEM((tm, D), w_gu.dtype), pltpu.VMEM((tm, D), F32)],
        compiler_params=_cparams("parallel", "arbitrary"),
        name="ffn",
    )(x, g.reshape(1, D), w_gu, w_gu, w_down)


def _top2_gates(logits, n_experts):
    lane = lax.broadcasted_iota(jnp.int32, logits.shape, 1).astype(F32)
    neg = jnp.float32(-jnp.inf)
    logits = jnp.where(lane < n_experts, logits, neg)
    m1 = jnp.max(logits, axis=-1, keepdims=True)
    i1 = jnp.min(jnp.where(logits == m1, lane, float(LANES)), axis=-1, keepdims=True)
    rest = jnp.where(lane == i1, neg, logits)
    m2 = jnp.max(rest, axis=-1, keepdims=True)
    i2 = jnp.min(jnp.where(rest == m2, lane, float(LANES)), axis=-1, keepdims=True)
    e2 = jnp.exp(m2 - m1)
    den = 1.0 + e2
    return jnp.where(lane == i1, 1.0 / den, 0.0) + jnp.where(lane == i2, e2 / den, 0.0)


def _moe_body(x_ref, g_ref, wr_ref, wg_ref, wu_ref, wd_ref, o_ref, h_ref, acc_ref, gate_ref, *, n_experts):
    e = pl.program_id(1)

    @pl.when(e == 0)
    def _():
        h = _rms(x_ref[...]) * g_ref[...]
        h_ref[...] = h.astype(BF16)
        acc_ref[...] = jnp.zeros_like(acc_ref)
        gate_ref[...] = _top2_gates(_mm(h, wr_ref[...]), n_experts)

    h = h_ref[...]
    a = _silu(jnp.dot(h, wg_ref[...], preferred_element_type=F32)) * jnp.dot(h, wu_ref[...], preferred_element_type=F32)
    gates = gate_ref[...]
    lane = lax.broadcasted_iota(jnp.int32, gates.shape, 1)
    gate = jnp.sum(jnp.where(lane == e, gates, 0.0), axis=-1, keepdims=True)
    acc_ref[...] += gate * jnp.dot(a.astype(BF16), wd_ref[...], preferred_element_type=F32)

    @pl.when(e == pl.num_programs(1) - 1)
    def _():
        o_ref[...] = x_ref[...] + acc_ref[...]


def moe_dense(x, g, w_router, w_gu, w_down, tm):
    M, D = x.shape
    E, F, _ = w_down.shape
    w_router = jnp.pad(w_router, ((0, 0), (0, LANES - E)))
    return pl.pallas_call(
        functools.partial(_moe_body, n_experts=E),
        grid=(M // tm, E),
        in_specs=[
            pl.BlockSpec((tm, D), lambda i, e: (i, 0)),
            pl.BlockSpec((1, D), lambda i, e: (0, 0)),
            pl.BlockSpec((D, LANES), lambda i, e: (0, 0)),
            pl.BlockSpec((None, D, F), lambda i, e: (e, 0, 0)),
            pl.BlockSpec((None, D, F), lambda i, e: (e, 0, 1)),
            pl.BlockSpec((None, F, D), lambda i, e: (e, 0, 0)),
        ],
        out_specs=pl.BlockSpec((tm, D), lambda i, e: (i, 0)),
        out_shape=jax.ShapeDtypeStruct((M, D), F32),
        scratch_shapes=[pltpu.VMEM((tm, D), BF16), pltpu.VMEM((tm, D), F32), pltpu.VMEM((tm, LANES), F32)],
        compiler_params=_cparams("parallel", "arbitrary"),
        name="moe",
    )(x, g.reshape(1, D), w_router, w_gu, w_gu, w_down)


def _log2_terms(z2):
    sign_bit = jnp.int32(-2 ** 31)
    neg_abs = lax.bitcast_convert_type(lax.bitcast_convert_type(z2, jnp.int32) | sign_bit, F32)
    nlom = jnp.maximum(z2, 0.0) + jnp.log2(1.0 + jnp.exp2(neg_abs))
    return nlom, z2 - nlom


MASKED_LOG2 = -1e30


def _sb_prompt_body(q_ref, k_ref, v_ref, b_ref, u_ref, o_ref, acc_ref, r_ref, lom_ref, lb_ref, *, tk, rq):
    i = pl.program_id(2)
    tq = rq * tk
    n = rq * (i + 1)
    bias = b_ref[...]
    acc_ref[...] = jnp.zeros_like(acc_ref)
    r_ref[...] = jnp.zeros_like(r_ref)

    def keys(ref, kb):
        return ref[pl.ds(pl.multiple_of(kb * tk, tk), tk), :]

    def stage_a(kb, slot, masked, r0=0):
        z = lax.dot_general(q_ref[r0:, :], keys(k_ref, kb), NT, preferred_element_type=F32) + bias
        nlom, lbeta = _log2_terms(z)
        if masked:
            row = lax.broadcasted_iota(jnp.int32, (tq - r0, tk), 0) + r0
            col = lax.broadcasted_iota(jnp.int32, (tq - r0, tk), 1)
            keep = col + kb * tk < row + i * tq
            nlom = jnp.where(keep, nlom, 0.0)
            lbeta = jnp.where(keep, lbeta, MASKED_LOG2)
        lom_ref[slot, r0:, :] = nlom.astype(BF16)
        lb_ref[slot, r0:, :] = lbeta

    def stage_b(kb, slot, r0=0):
        c = jnp.dot(lom_ref[slot, r0:, :], u_ref[...], preferred_element_type=F32)
        r = r_ref[r0:, :]
        a = jnp.exp2(lb_ref[slot, r0:, :] - c[:, :tk] - jnp.concatenate([r] * (tk // LANES), axis=1))
        acc_ref[r0:, :] += jnp.dot(a.astype(BF16), keys(v_ref, kb), preferred_element_type=F32)
        r_ref[r0:, :] = r + c[:, tk:]

    stage_a(n - 1, 0, True, (rq - 1) * tk)
    for t in range(1, rq):
        stage_a(n - 1 - t, t % 2, True, (rq - 1 - t) * tk)
        stage_b(n - t, (t - 1) % 2, (rq - t) * tk)

    def body(p, carry):
        kb = n - 1 - rq - 2 * p
        stage_a(kb, 0, False)
        stage_b(kb + 1, 1)
        stage_a(kb - 1, 1, False)
        stage_b(kb, 0)
        return carry

    lax.fori_loop(0, (n - rq) // 2, body, 0)
    stage_b(0, 1)
    o_ref[...] = acc_ref[...].astype(o_ref.dtype)


def _suffix_matrix(t):
    j = np.arange(t)[:, None]
    s = np.arange(t)[None, :]
    u = np.concatenate([(j > s).astype(np.float32), np.ones((t, LANES), np.float32)], axis=1)
    return jnp.asarray(u, dtype=BF16)


def sb_prompt(qkv, bias, B, S, tk, rq):
    W = qkv.shape[2]
    H = W // HEAD
    assert rq % 2 == 0
    tq = rq * tk
    nq = S // tq
    bias_b = jnp.broadcast_to((bias.astype(F32) * LOG2E)[:, None, None], (H, 1, tk))
    return pl.pallas_call(
        functools.partial(_sb_prompt_body, tk=tk, rq=rq),
        grid=(B, H, nq),
        in_specs=[
            pl.BlockSpec((None, tq, HEAD), lambda b, h, i: (0, b * nq + i, h)),
            pl.BlockSpec((None, S, HEAD), lambda b, h, i: (1, b, h)),
            pl.BlockSpec((None, S, HEAD), lambda b, h, i: (2, b, h)),
            pl.BlockSpec((None, 1, tk), lambda b, h, i: (h, 0, 0)),
            pl.BlockSpec((tk, tk + LANES), lambda b, h, i: (0, 0)),
        ],
        out_specs=pl.BlockSpec((tq, HEAD), lambda b, h, i: (b * nq + i, h)),
        out_shape=jax.ShapeDtypeStruct((B * S, W), BF16),
        scratch_shapes=[pltpu.VMEM((tq, HEAD), F32), pltpu.VMEM((tq, LANES), F32),
                        pltpu.VMEM((2, tq, tk), BF16), pltpu.VMEM((2, tq, tk), F32)],
        compiler_params=_cparams("parallel", "parallel", "arbitrary"),
        name="sb_prompt",
    )(qkv, qkv, qkv, bias_b, _suffix_matrix(tk))


def _sb_sample_body(pt_ref, q_ref, kn_ref, vn_ref, b_ref, u_ref, *rest, n_heads, ppb, t_valid):
    kp_refs = rest[:ppb]
    vp_refs = rest[ppb:2 * ppb]
    o_ref, qx_ref, acc_ref, r_ref, nl_ref, lb_ref, cs_ref = rest[2 * ppb:]
    del pt_ref
    pg = pl.program_id(1)
    G = LANES // n_heads
    bias = b_ref[...]

    def stage_a(get_k, npages, masked):
        z = bias
        for h in range(n_heads):
            k_h = jnp.concatenate([get_k(n, h) for n in range(npages)], axis=0)
            z = z + lax.dot_general(k_h, qx_ref[h], NT, preferred_element_type=F32)
        nlom, lbeta = _log2_terms(z)
        if masked:
            col = lax.broadcasted_iota(jnp.int32, z.shape, 1)
            row = lax.broadcasted_iota(jnp.int32, z.shape, 0)
            t = col & (G - 1)
            keep = (row < t) & (t < t_valid)
            nlom = jnp.where(keep, nlom, 0.0)
            lbeta = jnp.where(keep, lbeta, MASKED_LOG2)
        nlom_p = [nlom[n * PAGE:(n + 1) * PAGE] for n in range(npages)]
        sums = jnp.concatenate([jnp.sum(x, axis=0, keepdims=True) for x in nlom_p], axis=0)
        return jnp.concatenate(nlom_p, axis=1).astype(BF16), lbeta, sums

    def stage_b(nl, lbeta, sums, get_v, npages):
        after = jnp.dot(u_ref[...], nl, preferred_element_type=F32)
        r = r_ref[...]
        a_t = []
        for n in range(npages):
            a = jnp.exp2(lbeta[n * PAGE:(n + 1) * PAGE] - after[:, n * LANES:(n + 1) * LANES] - r)
            a_t.append(a.T.astype(BF16))
            r = r + sums[n:n + 1]
        r_ref[...] = r
        a_t = jnp.concatenate(a_t, axis=1)
        for h in range(n_heads):
            v_h = jnp.concatenate([get_v(n, h) for n in range(npages)], axis=0)
            acc_ref[h] += jnp.dot(a_t[h * G:(h + 1) * G], v_h, preferred_element_type=F32)

    def pad_rows(x, before, total):
        parts = []
        if before:
            parts.append(jnp.zeros((before, x.shape[1]), x.dtype))
        parts.append(x)
        if total - before - x.shape[0]:
            parts.append(jnp.zeros((total - before - x.shape[0], x.shape[1]), x.dtype))
        return jnp.concatenate(parts, axis=0)

    @pl.when(pg == 0)
    def _():
        for h in range(n_heads):
            sl = slice(h * HEAD, (h + 1) * HEAD)
            qx_ref[h] = pad_rows(q_ref[:, sl] * (LOG2E * HEAD ** -0.5), h * G, LANES).astype(BF16)
        acc_ref[...] = jnp.zeros_like(acc_ref)
        r_ref[...] = jnp.zeros_like(r_ref)

        def new_rows(ref):
            return lambda n, h: pad_rows(ref[:, h * HEAD:(h + 1) * HEAD], 0, PAGE).astype(BF16)

        stage_b(*stage_a(new_rows(kn_ref), 1, True), new_rows(vn_ref), 1)
        nl_ref[1] = jnp.zeros(nl_ref.shape[1:], nl_ref.dtype)
        lb_ref[1] = jnp.full(lb_ref.shape[1:], MASKED_LOG2, lb_ref.dtype)
        cs_ref[1] = jnp.zeros(cs_ref.shape[1:], cs_ref.dtype)

    def pool_rows(refs):
        return lambda n, h: refs[n][pl.ds(h, PAGE, stride=n_heads), :].astype(BF16)

    done = lax.rem(pg + 1, 2)
    stage_b(nl_ref[done], lb_ref[done], cs_ref[done], pool_rows(vp_refs), ppb)
    cur = lax.rem(pg, 2)
    nl, lbeta, sums = stage_a(pool_rows(kp_refs), ppb, False)
    nl_ref[cur] = nl
    lb_ref[cur] = lbeta
    cs_ref[cur] = sums

    @pl.when(pg == pl.num_programs(1) - 1)
    def _():
        for h in range(n_heads):
            o_ref[:, h * HEAD:(h + 1) * HEAD] = acc_ref[h][:T_PAD].astype(o_ref.dtype)


def sb_sample(proj, bias, pool_k, pool_v, layer, page_table, t_valid, ppb, out_dtype):
    W = proj.shape[2]
    H = W // HEAD
    G = LANES // H
    B, n_pages = page_table.shape
    npg = n_pages // ppb
    bias_b = jnp.repeat(bias.astype(F32) * LOG2E, G).reshape(1, LANES)
    j = np.arange(PAGE)
    u_t = jnp.asarray((j[None, :] > j[:, None]).astype(np.float32), dtype=BF16)

    def rows_view(pool):
        return pool.reshape(pool.shape[0], pool.shape[1], PAGE * H, HEAD)

    def tok(g):
        return pl.BlockSpec((None, T_PAD, W), lambda b, p, pt: (g, b, 0))

    def pool_spec(n, lag):
        def index(b, p, pt):
            grp = jnp.clip(p - lag, 0, npg - 1)
            return layer, pt[b, n_pages - 1 - (grp * ppb + n)], 0, 0

        return pl.BlockSpec((None, None, PAGE * H, HEAD), index)

    grid_spec = pltpu.PrefetchScalarGridSpec(
        num_scalar_prefetch=1,
        grid=(B, npg + 1),
        in_specs=[tok(0), tok(1), tok(2),
                  pl.BlockSpec((1, LANES), lambda b, p, pt: (0, 0)),
                  pl.BlockSpec((PAGE, PAGE), lambda b, p, pt: (0, 0))]
        + [pool_spec(n, 0) for n in range(ppb)] + [pool_spec(n, 1) for n in range(ppb)],
        out_specs=pl.BlockSpec((T_PAD, W), lambda b, p, pt: (b, 0)),
        scratch_shapes=[pltpu.VMEM((H, LANES, HEAD), BF16), pltpu.VMEM((H, G, HEAD), F32),
                        pltpu.VMEM((1, LANES), F32),
                        pltpu.VMEM((2, PAGE, ppb * LANES), BF16), pltpu.VMEM((2, ppb * PAGE, LANES), F32),
                        pltpu.VMEM((2, ppb, LANES), F32)],
    )
    return pl.pallas_call(
        functools.partial(_sb_sample_body, n_heads=H, ppb=ppb, t_valid=t_valid),
        grid_spec=grid_spec,
        out_shape=jax.ShapeDtypeStruct((B * T_PAD, W), out_dtype),
        compiler_params=_cparams("parallel", "arbitrary"),
        name="sb_sample",
    )(page_table, proj, proj, proj, bias_b, u_t, *([rows_view(pool_k)] * ppb), *([rows_view(pool_v)] * ppb))


def _rope_tables(pos):
    half = HEAD // 2
    inv = 1.0 / (ROPE_BASE ** jnp.linspace(0.0, 1.0, half, dtype=F32))
    ang = pos[:, None] * inv[None, :]
    cos, sin = jnp.cos(ang), jnp.sin(ang)
    return jnp.concatenate([cos, cos], axis=-1), jnp.concatenate([-sin, sin], axis=-1)


def _retention_tables(n_heads, chunk):
    log_g = np.log(1.0 - 2.0 ** (-5.0 - np.arange(n_heads, dtype=np.float64)))
    idx = np.arange(HEAD, dtype=np.float64)
    diff = idx[:, None] - idx[None, :]
    dmask = np.where(diff >= 0, np.exp(log_g[:, None, None] * np.maximum(diff, 0.0)), 0.0)
    qdec = np.exp(log_g[:, None] * (idx + 1.0)[None, :])
    kdec = np.where(idx[None, :] < chunk, np.exp(log_g[:, None] * (chunk - 1.0 - idx)[None, :]), 0.0)
    bc = lambda t: jnp.asarray(np.broadcast_to(t[:, :, None], (n_heads, HEAD, HEAD)), dtype=F32)
    gch = tuple(float(x) for x in np.exp(log_g * chunk))
    return jnp.asarray(dmask, dtype=F32), bc(qdec), bc(kdec), gch


def kernel(x_prompt, x_sample, mem_prompt, state_ret, cache_sb_k, cache_sb_v, page_table, cache_mem_k, cache_mem_v, norm_mix_g, norm_x_g, norm_ffn_g, w_in_even, w_out_even, sg_norm_g, sg_w_s, sg_b, w_qkv_sb, w_o_sb, sb_bias, w_q_x, w_k_x, w_v_x, w_o_x, q_norm_x_g, k_norm_x_g, w_gu_dense, w_down_dense, w_router, w_gu_moe, w_down_moe):
    B, S, D = x_prompt.shape
    BS, T, _ = x_sample.shape
    depth = norm_mix_g.shape[0]
    n_heads_ret = state_ret.shape[2]
    n_experts = w_router.shape[2]
    past = page_table.shape[1] * PAGE
    H_sb = cache_sb_k.shape[3]
    W_sb = H_sb * HEAD
    W_x = w_q_x.shape[2]

    bf = lambda w: w.astype(BF16)
    TM = min(1024, B * S)
    MS = BS * T_PAD

    xp = x_prompt.reshape(B * S, D)
    xs = jnp.pad(x_sample, ((0, 0), (0, T_PAD - T), (0, 0))).reshape(MS, D)

    cos_p, sin_p = _rope_tables(jnp.arange(S, dtype=F32))
    cos_s, sin_s = _rope_tables(past + jnp.arange(HEAD, dtype=F32))
    dmask, qdec, kdec_p, gch_p = _retention_tables(n_heads_ret, HEAD)
    _, _, kdec_s, gch_s = _retention_tables(n_heads_ret, T)

    mk_p, mv_p, mk_p16, mv_p16 = mem_kv(mem_prompt, bf(w_k_x), bf(w_v_x), k_norm_x_g)
    n_f32 = max(depth - 2, 0)
    head_major = lambda c: jnp.transpose(c, (0, 1, 3, 2, 4))
    cmk, cmv = head_major(cache_mem_k[:n_f32]), head_major(cache_mem_v[:n_f32])
    cmk16, cmv16 = bf(head_major(cache_mem_k[n_f32:])), bf(head_major(cache_mem_v[n_f32:]))

    ret_p, ret_s, sgv_s = [], [], []
    kv_p = None
    sbk_s, sbv_s = [], []
    for l in range(depth):
        i = l // 2
        sample_f32 = l + 2 < depth
        if l % 2 == 0:
            w_in, w_out = bf(w_in_even[i]), bf(w_out_even[i])
            sgb = jnp.broadcast_to(sg_b[i][:, :, None], sg_b[i].shape + (HEAD,))
            proj = norm_mm(xp, norm_mix_g[l], w_in, 6, TM, 512)
            y, sp = even_core(proj, B, S, HEAD, cos_p, sin_p, dmask, qdec, kdec_p, gch_p,
                              sg_norm_g[i], sg_w_s[i], sgb, None, False, BF16)
            mix_p = (y, w_out)
            if sample_f32:
                w_in, w_out = w_in_even[i], w_out_even[i]
            proj = norm_mm(xs, norm_mix_g[l], w_in, 6, MS, 512)
            y, ss, vs = even_core(proj, BS, T_PAD, T_PAD, cos_s, sin_s, dmask, qdec, kdec_s, gch_s,
                                  sg_norm_g[i], sg_w_s[i], sgb, state_ret[i], True, w_in.dtype)
            xs = mm_res(y, w_out, xs, MS, 512)
            ret_p.append(sp)
            ret_s.append(ss)
            sgv_s.append(vs.reshape(BS, T_PAD, -1)[:, :T])
        else:
            w_qkv, w_o = bf(w_qkv_sb[i]), bf(w_o_sb[i])
            qkv16, *kv_p = norm_qkv(xp, norm_mix_g[l], w_qkv, LOG2E * HEAD ** -0.5, i, depth // 2, kv_p,
                                    TM)
            o = sb_prompt(qkv16, sb_bias[i], B, S, 256, 2)
            mix_p = (o, w_o)
            if sample_f32:
                w_qkv, w_o = w_qkv_sb[i], w_o_sb[i]
            proj = norm_mm(xs, norm_mix_g[l], w_qkv, 3, MS, 512)
            o = sb_sample(proj, sb_bias[i], cache_sb_k, cache_sb_v, i, page_table, T, math.gcd(16, page_table.shape[1]), w_o.dtype)
            xs = mm_res(o, w_o, xs, MS, 512)
            sbk_s.append(proj[1].reshape(BS, T_PAD, H_sb, HEAD)[:, :T])
            sbv_s.append(proj[2].reshape(BS, T_PAD, H_sb, HEAD)[:, :T])
        wq, wo = bf(w_q_x[l]), bf(w_o_x[l])
        xp = cross_attn(mix_p[0].reshape(B, S, -1), mix_p[1], xp.reshape(B, S, D), norm_x_g[l], wq, q_norm_x_g[l],
                        mk_p16, mv_p16, l, wo, 512).reshape(B * S, D)
        mem_s = (cmk16, cmv16, l - n_f32)
        if sample_f32:
            wq, wo, mem_s = w_q_x[l], w_o_x[l], (cmk, cmv, l)
        q_s = norm_mm(xs, norm_x_g[l], wq, 1, MS, 512)
        o_s = mem_attn(q_s.reshape(BS, T_PAD, -1), q_norm_x_g[l], *mem_s)
        xs = mm_res(o_s.reshape(MS, -1), wo, xs, MS, 512)
        if l % 2 == 0:
            w_gu, w_down = bf(w_gu_dense[i]), bf(w_down_dense[i])
            xp = ffn(xp, norm_ffn_g[l], w_gu, w_down, TM, 256)
            if sample_f32:
                w_gu, w_down = w_gu_dense[i], w_down_dense[i]
            xs = ffn(xs, norm_ffn_g[l], w_gu, w_down, MS, 256)
        else:
            w_gu, w_down = bf(w_gu_moe[i]), bf(w_down_moe[i])
            xp = moe_dense(xp, norm_ffn_g[l], w_router[i], w_gu, w_down, min(512, B * S))
            xs = moe_dense(xs, norm_ffn_g[l], w_router[i], w_gu, w_down, MS)

    y_p = xp.reshape(B, S, D)
    y_s = xs.reshape(BS, T_PAD, D)[:, :T]
    return (y_p, y_s, jnp.stack(ret_p), jnp.stack(ret_s), jnp.stack(sgv_s),
            kv_p[0].reshape(-1, B, S, H_sb, HEAD), kv_p[1].reshape(-1, B, S, H_sb, HEAD),
            jnp.stack(sbk_s), jnp.stack(sbv_s), mk_p, mv_p)
```

```python
import functools
import math

import numpy as np
import jax
import jax.numpy as jnp
from jax import lax
from jax.experimental import pallas as pl
from jax.experimental.pallas import tpu as pltpu

F32 = jnp.float32
BF16 = jnp.bfloat16
EPS = 1e-6
ROPE_BASE = 10000.0
LOG2E = math.log2(math.e)
LANES = 128
VMEM_LIMIT_BYTES = 56 * 1024 * 1024
HEAD = 128
PAGE = 128
T_PAD = 8
NN = (((1,), (0,)), ((), ()))
NT = (((1,), (1,)), ((), ()))
TN = (((0,), (0,)), ((), ()))


def _cparams(*sem):
    return pltpu.CompilerParams(dimension_semantics=sem, vmem_limit_bytes=VMEM_LIMIT_BYTES)


def _rms(x):
    return x * lax.rsqrt(jnp.mean(x * x, axis=-1, keepdims=True) + EPS)


def _gelu(x):
    return 0.5 * x * (1.0 + lax.erf(x * (2.0 ** -0.5)))


def _silu(x):
    return x * jax.nn.sigmoid(x)


def _mm(a, b, dims=NN):
    assert a.dtype == b.dtype
    precision = lax.Precision.HIGHEST if a.dtype == F32 else None
    return lax.dot_general(a, b, dims, precision=precision, preferred_element_type=F32)


def _norm_mm_body(x_ref, g_ref, w_ref, o_ref, h_ref):
    @pl.when(pl.program_id(1) == 0)
    def _():
        h_ref[...] = (_rms(x_ref[...]) * g_ref[...]).astype(h_ref.dtype)

    o_ref[...] = _mm(h_ref[...], w_ref[...])


def norm_mm(x, g, w, groups, tm, tn):
    M, K = x.shape
    N = w.shape[1]
    ng = N // groups
    npg = ng // tn
    return pl.pallas_call(
        _norm_mm_body,
        grid=(M // tm, N // tn),
        in_specs=[
            pl.BlockSpec((tm, K), lambda i, j: (i, 0)),
            pl.BlockSpec((1, K), lambda i, j: (0, 0)),
            pl.BlockSpec((K, tn), lambda i, j: (0, j)),
        ],
        out_specs=pl.BlockSpec((None, tm, tn), lambda i, j: (j // npg, i, j % npg)),
        out_shape=jax.ShapeDtypeStruct((groups, M, ng), F32),
        scratch_shapes=[pltpu.VMEM((tm, K), w.dtype)],
        compiler_params=_cparams("parallel", "arbitrary"),
        name="norm_mm",
    )(x, g.reshape(1, K), w)


def _norm_qkv_body(x_ref, g_ref, w_ref, *rest, q_scale, n_heads, n_alias):
    qkv_ref, k_ref, v_ref, h_ref = rest[n_alias:]
    grp = pl.program_id(1)

    @pl.when(grp == 0)
    def _():
        h_ref[...] = (_rms(x_ref[...]) * g_ref[...]).astype(h_ref.dtype)

    o = _mm(h_ref[...], w_ref[...])
    qkv_ref[...] = (o * jnp.where(grp == 0, q_scale, 1.0)).astype(qkv_ref.dtype)

    def heads_out(ref):
        for h in range(n_heads):
            ref[:, h, :] = o[:, h * HEAD:(h + 1) * HEAD]

    pl.when(grp == 1)(lambda: heads_out(k_ref))
    pl.when(grp == 2)(lambda: heads_out(v_ref))


def norm_qkv(x, g, w, q_scale, layer_slot, n_slots, kv_prev, tm):
    M, K = x.shape
    W = w.shape[1] // 3
    nh = W // HEAD
    kv_shape = jax.ShapeDtypeStruct((n_slots, M, nh, HEAD), F32)
    kv_spec = pl.BlockSpec((None, tm, nh, HEAD), lambda i, j: (layer_slot, i, 0, 0))
    n_alias = 0 if kv_prev is None else 2
    return pl.pallas_call(
        functools.partial(_norm_qkv_body, q_scale=q_scale, n_heads=nh, n_alias=n_alias),
        grid=(M // tm, 3),
        in_specs=[
            pl.BlockSpec((tm, K), lambda i, j: (i, 0)),
            pl.BlockSpec((1, K), lambda i, j: (0, 0)),
            pl.BlockSpec((K, W), lambda i, j: (0, j)),
        ] + [pl.BlockSpec(memory_space=pl.ANY)] * n_alias,
        out_specs=[pl.BlockSpec((None, tm, W), lambda i, j: (j, i, 0)), kv_spec, kv_spec],
        out_shape=[jax.ShapeDtypeStruct((3, M, W), BF16), kv_shape, kv_shape],
        input_output_aliases={3: 1, 4: 2} if n_alias else {},
        scratch_shapes=[pltpu.VMEM((tm, K), w.dtype)],
        compiler_params=_cparams("parallel", "arbitrary"),
        name="norm_qkv",
    )(x, g.reshape(1, K), w, *(kv_prev or ()))


def _mm_res_body(a_ref, w_ref, r_ref, o_ref):
    o_ref[...] = r_ref[...] + _mm(a_ref[...], w_ref[...])


def mm_res(a, w, res, tm, tn):
    M, K = a.shape
    N = w.shape[1]
    tn = min(tn, N)
    return pl.pallas_call(
        _mm_res_body,
        grid=(M // tm, N // tn),
        in_specs=[
            pl.BlockSpec((tm, K), lambda i, j: (i, 0)),
            pl.BlockSpec((K, tn), lambda i, j: (0, j)),
            pl.BlockSpec((tm, tn), lambda i, j: (i, j)),
        ],
        out_specs=pl.BlockSpec((tm, tn), lambda i, j: (i, j)),
        out_shape=jax.ShapeDtypeStruct((M, N), F32),
        compiler_params=_cparams("parallel", "arbitrary"),
        name="mm_res",
    )(a, w, res)


def _even_core_body(*refs, n_heads, n_groups, gch, rows_in, has_s0, emit_vb):
    it = iter(refs)
    p_ref, cos_ref, sin_ref, dmask_ref, qdec_ref, kdec_ref, sgg_ref, ws_ref, sgb_ref = (next(it) for _ in range(9))
    s0_ref = next(it) if has_s0 else None
    y_ref, sout_ref = next(it), next(it)
    vb_ref = next(it) if emit_vb else None
    s_ref = next(it)
    C = HEAD
    cd = y_ref.dtype
    c = pl.program_id(1)
    w_ret = n_heads * HEAD

    @pl.when(c == 0)
    def _():
        if has_s0:
            s_ref[...] = s0_ref[...]
        else:
            s_ref[...] = jnp.zeros_like(s_ref)

    def rows(x):
        if rows_in == C:
            return x
        return jnp.concatenate([x, jnp.zeros((C - rows_in, x.shape[1]), x.dtype)], axis=0)

    cos = cos_ref[...]
    sin = sin_ref[...]

    def rot(x):
        return x * cos + pltpu.roll(x, HEAD // 2, 1) * sin

    for h in range(n_heads):
        sl = slice(h * HEAD, (h + 1) * HEAD)
        q = rot(rows(p_ref[0, :, sl]))
        k = rot(rows(p_ref[1, :, sl])) * (HEAD ** -0.5)
        v = rows(p_ref[2, :, sl]).astype(cd)
        g = rows(p_ref[3, :, sl])
        intra = _mm(q.astype(cd), k.astype(cd), NT) * dmask_ref[h]
        o = _mm(intra.astype(cd), v)
        s = s_ref[h]
        o = o + _mm((q * qdec_ref[h]).astype(cd), s.astype(cd))
        kv = _mm((k * kdec_ref[h]).astype(cd), v, TN)
        s_ref[h] = gch[h] * s + kv
        oa = _rms(o) * _silu(g)
        y_ref[:, sl] = oa[:rows_in].astype(y_ref.dtype)

    u = _gelu(rows(p_ref[4]))
    vg = _gelu(rows(p_ref[5]))
    xc = vg - jnp.mean(vg, axis=-1, keepdims=True)
    vb = xc * lax.rsqrt(jnp.mean(xc * xc, axis=-1, keepdims=True) + EPS) * sgg_ref[...]
    if emit_vb:
        vb_ref[...] = vb[:rows_in]
    row = lax.broadcasted_iota(jnp.int32, (C, C), 0)
    col = lax.broadcasted_iota(jnp.int32, (C, C), 1)
    for gi in range(n_groups):
        sl = slice(gi * HEAD, (gi + 1) * HEAD)
        w = jnp.where(col <= row, ws_ref[gi], 0.0).astype(cd)
        f = _mm(w, vb[:, sl].astype(cd)) + sgb_ref[gi]
        ob = u[:, sl] * f
        y_ref[:, w_ret + gi * HEAD:w_ret + (gi + 1) * HEAD] = ob[:rows_in].astype(y_ref.dtype)

    @pl.when(c == pl.num_programs(1) - 1)
    def _():
        sout_ref[...] = s_ref[...]


def even_core(proj, B, L, rows_in, cos, sin, dmask, qdec, kdec, gch, sg_g, w_s, sgb, s0, emit_vb, out_dtype):
    n_heads = dmask.shape[0]
    n_groups = w_s.shape[0]
    W = proj.shape[2]
    nc = L // rows_in
    has_s0 = s0 is not None
    const3 = lambda b, c: (0, 0, 0)
    in_specs = [
        pl.BlockSpec((6, rows_in, W), lambda b, c: (0, b * nc + c, 0)),
        pl.BlockSpec((HEAD, HEAD), lambda b, c: (c, 0)),
        pl.BlockSpec((HEAD, HEAD), lambda b, c: (c, 0)),
        pl.BlockSpec(dmask.shape, const3),
        pl.BlockSpec(qdec.shape, const3),
        pl.BlockSpec(kdec.shape, const3),
        pl.BlockSpec((1, W), lambda b, c: (0, 0)),
        pl.BlockSpec(w_s.shape, const3),
        pl.BlockSpec(sgb.shape, const3),
    ]
    args = [proj, cos, sin, dmask, qdec, kdec, sg_g.reshape(1, W), w_s, sgb]
    if has_s0:
        in_specs.append(pl.BlockSpec((None, n_heads, HEAD, HEAD), lambda b, c: (b, 0, 0, 0)))
        args.append(s0)
    out_specs = [
        pl.BlockSpec((rows_in, 2 * W), lambda b, c: (b * nc + c, 0)),
        pl.BlockSpec((None, n_heads, HEAD, HEAD), lambda b, c: (b, 0, 0, 0)),
    ]
    out_shape = [
        jax.ShapeDtypeStruct((B * L, 2 * W), out_dtype),
        jax.ShapeDtypeStruct((B, n_heads, HEAD, HEAD), F32),
    ]
    if emit_vb:
        out_specs.append(pl.BlockSpec((rows_in, W), lambda b, c: (b * nc + c, 0)))
        out_shape.append(jax.ShapeDtypeStruct((B * L, W), F32))
    body = functools.partial(_even_core_body, n_heads=n_heads, n_groups=n_groups, gch=gch,
                             rows_in=rows_in, has_s0=has_s0, emit_vb=emit_vb)
    return pl.pallas_call(
        body,
        grid=(B, nc),
        in_specs=in_specs,
        out_specs=out_specs,
        out_shape=out_shape,
        scratch_shapes=[pltpu.VMEM((n_heads, HEAD, HEAD), F32)],
        compiler_params=_cparams("parallel", "arbitrary"),
        name="even_core",
    )(*args)


def _mem_kv_body(m_ref, wk_ref, wv_ref, kg_ref, k_ref, v_ref, k16_ref, v16_ref, *, n_heads):
    m = m_ref[...].astype(BF16)
    k = jnp.dot(m, wk_ref[...], preferred_element_type=F32)
    kg = kg_ref[...]
    v = jnp.dot(m, wv_ref[...], preferred_element_type=F32)
    for h in range(n_heads):
        sl = slice(h * HEAD, (h + 1) * HEAD)
        kn = _rms(k[:, sl]) * kg
        k_ref[:, h, :] = kn
        v_ref[:, h, :] = v[:, sl]
        k16_ref[h] = kn.astype(BF16)
        v16_ref[h] = v[:, sl].astype(BF16)


def mem_kv(mem, w_k, w_v, k_g):
    B, NM, D = mem.shape
    depth, _, WX = w_k.shape
    nh = WX // HEAD
    out = jax.ShapeDtypeStruct((depth, B, NM, nh, HEAD), F32)
    out16 = jax.ShapeDtypeStruct((depth, B, nh, NM, HEAD), BF16)
    return pl.pallas_call(
        functools.partial(_mem_kv_body, n_heads=WX // HEAD),
        grid=(depth, B),
        in_specs=[
            pl.BlockSpec((None, NM, D), lambda l, b: (b, 0, 0)),
            pl.BlockSpec((None, D, WX), lambda l, b: (l, 0, 0)),
            pl.BlockSpec((None, D, WX), lambda l, b: (l, 0, 0)),
            pl.BlockSpec((None, 1, HEAD), lambda l, b: (l, 0, 0)),
        ],
        out_specs=[pl.BlockSpec((None, None, NM, nh, HEAD), lambda l, b: (l, b, 0, 0, 0))] * 2
        + [pl.BlockSpec((None, None, nh, NM, HEAD), lambda l, b: (l, b, 0, 0, 0))] * 2,
        out_shape=[out, out, out16, out16],
        compiler_params=_cparams("parallel", "parallel"),
        name="mem_kv",
    )(mem, w_k, w_v, k_g.reshape(depth, 1, HEAD))


def _mem_attention(q, qg, mk_ref, mv_ref, cd):
    outs = []
    for h in range(mk_ref.shape[0]):
        qn = (_rms(q[:, h * HEAD:(h + 1) * HEAD]) * qg).astype(cd)
        s = _mm(qn, mk_ref[h], NT)
        e = jnp.exp2(s - jnp.max(s, axis=-1, keepdims=True))
        outs.append(_mm(e.astype(cd), mv_ref[h]) * (1.0 / jnp.sum(e, axis=-1, keepdims=True)))
    return jnp.concatenate(outs, axis=-1).astype(cd)


def _cross_attn_body(a_ref, wa_ref, x_ref, g_ref, wq_ref, qg_ref, mk_ref, mv_ref, wo_ref, o_ref):
    cd = wq_ref.dtype
    x = x_ref[...] + _mm(a_ref[...], wa_ref[...])
    q = _mm((_rms(x) * g_ref[...]).astype(cd), wq_ref[...])
    o_ref[...] = x + _mm(_mem_attention(q, qg_ref[...], mk_ref, mv_ref, cd), wo_ref[...])


def _mem_attn_body(q_ref, qg_ref, mk_ref, mv_ref, o_ref):
    o_ref[...] = _mem_attention(q_ref[...], qg_ref[...], mk_ref, mv_ref, o_ref.dtype)


def mem_attn(q, qg, mk_all, mv_all, layer):
    B, L, WX = q.shape
    nh, NM = mk_all.shape[2], mk_all.shape[3]
    mem_spec = pl.BlockSpec((None, None, nh, NM, HEAD), lambda b: (layer, b, 0, 0, 0))
    return pl.pallas_call(
        _mem_attn_body,
        grid=(B,),
        in_specs=[
            pl.BlockSpec((None, L, WX), lambda b: (b, 0, 0)),
            pl.BlockSpec((1, HEAD), lambda b: (0, 0)),
            mem_spec,
            mem_spec,
        ],
        out_specs=pl.BlockSpec((None, L, WX), lambda b: (b, 0, 0)),
        out_shape=jax.ShapeDtypeStruct((B, L, WX), mk_all.dtype),
        compiler_params=_cparams("parallel"),
        name="mem_attn",
    )(q, (qg * (LOG2E * HEAD ** -0.5)).reshape(1, HEAD), mk_all, mv_all)


def cross_attn(a, wa, x, g, wq, qg, mk_all, mv_all, layer, wo, tm):
    B, L, D = x.shape
    Ka = a.shape[2]
    nh, NM = mk_all.shape[2], mk_all.shape[3]
    WX = nh * HEAD
    qg = qg * (LOG2E * HEAD ** -0.5)
    mem_spec = pl.BlockSpec((None, None, nh, NM, HEAD), lambda b, i: (layer, b, 0, 0, 0))
    return pl.pallas_call(
        _cross_attn_body,
        grid=(B, L // tm),
        in_specs=[
            pl.BlockSpec((None, tm, Ka), lambda b, i: (b, i, 0)),
            pl.BlockSpec((Ka, D), lambda b, i: (0, 0)),
            pl.BlockSpec((None, tm, D), lambda b, i: (b, i, 0)),
            pl.BlockSpec((1, D), lambda b, i: (0, 0)),
            pl.BlockSpec((D, WX), lambda b, i: (0, 0)),
            pl.BlockSpec((1, HEAD), lambda b, i: (0, 0)),
            mem_spec,
            mem_spec,
            pl.BlockSpec((WX, D), lambda b, i: (0, 0)),
        ],
        out_specs=pl.BlockSpec((None, tm, D), lambda b, i: (b, i, 0)),
        out_shape=jax.ShapeDtypeStruct((B, L, D), F32),
        compiler_params=_cparams("parallel", "parallel"),
        name="cross_attn",
    )(a, wa, x, g.reshape(1, D), wq, qg.reshape(1, HEAD), mk_all, mv_all, wo)


def _ffn_body(x_ref, g_ref, wg_ref, wu_ref, wd_ref, o_ref, h_ref, acc_ref):
    f = pl.program_id(1)

    @pl.when(f == 0)
    def _():
        h_ref[...] = (_rms(x_ref[...]) * g_ref[...]).astype(h_ref.dtype)
        acc_ref[...] = jnp.zeros_like(acc_ref)

    h = h_ref[...]
    a = _silu(_mm(h, wg_ref[...])) * _mm(h, wu_ref[...])
    acc_ref[...] += _mm(a.astype(h.dtype), wd_ref[...])

    @pl.when(f == pl.num_programs(1) - 1)
    def _():
        o_ref[...] = x_ref[...] + acc_ref[...]


def ffn(x, g, w_gu, w_down, tm, tf):
    M, D = x.shape
    F = w_down.shape[0]
    nf = F // tf
    return pl.pallas_call(
        _ffn_body,
        grid=(M // tm, nf),
        in_specs=[
            pl.BlockSpec((tm, D), lambda i, f: (i, 0)),
            pl.BlockSpec((1, D), lambda i, f: (0, 0)),
            pl.BlockSpec((D, tf), lambda i, f: (0, f)),
            pl.BlockSpec((D, tf), lambda i, f: (0, nf + f)),
            pl.BlockSpec((tf, D), lambda i, f: (f, 0)),
        ],
        out_specs=pl.BlockSpec((tm, D), lambda i, f: (i, 0)),
        out_shape=jax.ShapeDtypeStruct((M, D), F32),
        scratch_shapes=[pltpu.VMEM((tm, D), w_gu.dtype), pltpu.VMEM((tm, D), F32)],
        compiler_params=_cparams("parallel", "arbitrary"),
        name="ffn",
    )(x, g.reshape(1, D), w_gu, w_gu, w_down)


def _top2_gates(logits, n_experts):
    lane = lax.broadcasted_iota(jnp.int32, logits.shape, 1).astype(F32)
    neg = jnp.float32(-jnp.inf)
    logits = jnp.where(lane < n_experts, logits, neg)
    m1 = jnp.max(logits, axis=-1, keepdims=True)
    i1 = jnp.min(jnp.where(logits == m1, lane, float(LANES)), axis=-1, keepdims=True)
    rest = jnp.where(lane == i1, neg, logits)
    m2 = jnp.max(rest, axis=-1, keepdims=True)
    i2 = jnp.min(jnp.where(rest == m2, lane, float(LANES)), axis=-1, keepdims=True)
    e2 = jnp.exp(m2 - m1)
    den = 1.0 + e2
    return jnp.where(lane == i1, 1.0 / den, 0.0) + jnp.where(lane == i2, e2 / den, 0.0)


def _moe_body(x_ref, g_ref, wr_ref, wg_ref, wu_ref, wd_ref, o_ref, h_ref, acc_ref, gate_ref, *, n_experts):
    e = pl.program_id(1)

    @pl.when(e == 0)
    def _():
        h = _rms(x_ref[...]) * g_ref[...]
        h_ref[...] = h.astype(BF16)
        acc_ref[...] = jnp.zeros_like(acc_ref)
        gate_ref[...] = _top2_gates(_mm(h, wr_ref[...]), n_experts)

    h = h_ref[...]
    a = _silu(jnp.dot(h, wg_ref[...], preferred_element_type=F32)) * jnp.dot(h, wu_ref[...], preferred_element_type=F32)
    gates = gate_ref[...]
    lane = lax.broadcasted_iota(jnp.int32, gates.shape, 1)
    gate = jnp.sum(jnp.where(lane == e, gates, 0.0), axis=-1, keepdims=True)
    acc_ref[...] += gate * jnp.dot(a.astype(BF16), wd_ref[...], preferred_element_type=F32)

    @pl.when(e == pl.num_programs(1) - 1)
    def _():
        o_ref[...] = x_ref[...] + acc_ref[...]


def moe_dense(x, g, w_router, w_gu, w_down, tm):
    M, D = x.shape
    E, F, _ = w_down.shape
    w_router = jnp.pad(w_router, ((0, 0), (0, LANES - E)))
    return pl.pallas_call(
        functools.partial(_moe_body, n_experts=E),
        grid=(M // tm, E),
        in_specs=[
            pl.BlockSpec((tm, D), lambda i, e: (i, 0)),
            pl.BlockSpec((1, D), lambda i, e: (0, 0)),
            pl.BlockSpec((D, LANES), lambda i, e: (0, 0)),
            pl.BlockSpec((None, D, F), lambda i, e: (e, 0, 0)),
            pl.BlockSpec((None, D, F), lambda i, e: (e, 0, 1)),
            pl.BlockSpec((None, F, D), lambda i, e: (e, 0, 0)),
        ],
        out_specs=pl.BlockSpec((tm, D), lambda i, e: (i, 0)),
        out_shape=jax.ShapeDtypeStruct((M, D), F32),
        scratch_shapes=[pltpu.VMEM((tm, D), BF16), pltpu.VMEM((tm, D), F32), pltpu.VMEM((tm, LANES), F32)],
        compiler_params=_cparams("parallel", "arbitrary"),
        name="moe",
    )(x, g.reshape(1, D), w_router, w_gu, w_gu, w_down)


def _log2_terms(z2):
    sign_bit = jnp.int32(-2 ** 31)
    neg_abs = lax.bitcast_convert_type(lax.bitcast_convert_type(z2, jnp.int32) | sign_bit, F32)
    nlom = jnp.maximum(z2, 0.0) + jnp.log2(1.0 + jnp.exp2(neg_abs))
    return nlom, z2 - nlom


MASKED_LOG2 = -1e30


def _sb_prompt_body(q_ref, k_ref, v_ref, b_ref, u_ref, o_ref, acc_ref, r_ref, lom_ref, lb_ref, *, tk, rq):
    i = pl.program_id(2)
    tq = rq * tk
    n = rq * (i + 1)
    bias = b_ref[...]
    acc_ref[...] = jnp.zeros_like(acc_ref)
    r_ref[...] = jnp.zeros_like(r_ref)

    def keys(ref, kb):
        return ref[pl.ds(pl.multiple_of(kb * tk, tk), tk), :]

    def stage_a(kb, slot, masked, r0=0):
        z = lax.dot_general(q_ref[r0:, :], keys(k_ref, kb), NT, preferred_element_type=F32) + bias
        nlom, lbeta = _log2_terms(z)
        if masked:
            row = lax.broadcasted_iota(jnp.int32, (tq - r0, tk), 0) + r0
            col = lax.broadcasted_iota(jnp.int32, (tq - r0, tk), 1)
            keep = col + kb * tk < row + i * tq
            nlom = jnp.where(keep, nlom, 0.0)
            lbeta = jnp.where(keep, lbeta, MASKED_LOG2)
        lom_ref[slot, r0:, :] = nlom.astype(BF16)
        lb_ref[slot, r0:, :] = lbeta

    def stage_b(kb, slot, r0=0):
        c = jnp.dot(lom_ref[slot, r0:, :], u_ref[...], preferred_element_type=F32)
        r = r_ref[r0:, :]
        a = jnp.exp2(lb_ref[slot, r0:, :] - c[:, :tk] - jnp.concatenate([r] * (tk // LANES), axis=1))
        acc_ref[r0:, :] += jnp.dot(a.astype(BF16), keys(v_ref, kb), preferred_element_type=F32)
        r_ref[r0:, :] = r + c[:, tk:]

    stage_a(n - 1, 0, True, (rq - 1) * tk)
    for t in range(1, rq):
        stage_a(n - 1 - t, t % 2, True, (rq - 1 - t) * tk)
        stage_b(n - t, (t - 1) % 2, (rq - t) * tk)

    def body(p, carry):
        kb = n - 1 - rq - 2 * p
        stage_a(kb, 0, False)
        stage_b(kb + 1, 1)
        stage_a(kb - 1, 1, False)
        stage_b(kb, 0)
        return carry

    lax.fori_loop(0, (n - rq) // 2, body, 0)
    stage_b(0, 1)
    o_ref[...] = acc_ref[...].astype(o_ref.dtype)


def _suffix_matrix(t):
    j = np.arange(t)[:, None]
    s = np.arange(t)[None, :]
    u = np.concatenate([(j > s).astype(np.float32), np.ones((t, LANES), np.float32)], axis=1)
    return jnp.asarray(u, dtype=BF16)


def sb_prompt(qkv, bias, B, S, tk, rq):
    W = qkv.shape[2]
    H = W // HEAD
    assert rq % 2 == 0
    tq = rq * tk
    nq = S // tq
    bias_b = jnp.broadcast_to((bias.astype(F32) * LOG2E)[:, None, None], (H, 1, tk))
    return pl.pallas_call(
        functools.partial(_sb_prompt_body, tk=tk, rq=rq),
        grid=(B, H, nq),
        in_specs=[
            pl.BlockSpec((None, tq, HEAD), lambda b, h, i: (0, b * nq + i, h)),
            pl.BlockSpec((None, S, HEAD), lambda b, h, i: (1, b, h)),
            pl.BlockSpec((None, S, HEAD), lambda b, h, i: (2, b, h)),
            pl.BlockSpec((None, 1, tk), lambda b, h, i: (h, 0, 0)),
            pl.BlockSpec((tk, tk + LANES), lambda b, h, i: (0, 0)),
        ],
        out_specs=pl.BlockSpec((tq, HEAD), lambda b, h, i: (b * nq + i, h)),
        out_shape=jax.ShapeDtypeStruct((B * S, W), BF16),
        scratch_shapes=[pltpu.VMEM((tq, HEAD), F32), pltpu.VMEM((tq, LANES), F32),
                        pltpu.VMEM((2, tq, tk), BF16), pltpu.VMEM((2, tq, tk), F32)],
        compiler_params=_cparams("parallel", "parallel", "arbitrary"),
        name="sb_prompt",
    )(qkv, qkv, qkv, bias_b, _suffix_matrix(tk))


def _sb_sample_body(pt_ref, q_ref, kn_ref, vn_ref, b_ref, u_ref, *rest, n_heads, ppb, t_valid):
    kp_refs = rest[:ppb]
    vp_refs = rest[ppb:2 * ppb]
    o_ref, qx_ref, acc_ref, r_ref, nl_ref, lb_ref, cs_ref = rest[2 * ppb:]
    del pt_ref
    pg = pl.program_id(1)
    G = LANES // n_heads
    bias = b_ref[...]

    def stage_a(get_k, npages, masked):
        z = bias
        for h in range(n_heads):
            k_h = jnp.concatenate([get_k(n, h) for n in range(npages)], axis=0)
            z = z + lax.dot_general(k_h, qx_ref[h], NT, preferred_element_type=F32)
        nlom, lbeta = _log2_terms(z)
        if masked:
            col = lax.broadcasted_iota(jnp.int32, z.shape, 1)
            row = lax.broadcasted_iota(jnp.int32, z.shape, 0)
            t = col & (G - 1)
            keep = (row < t) & (t < t_valid)
            nlom = jnp.where(keep, nlom, 0.0)
            lbeta = jnp.where(keep, lbeta, MASKED_LOG2)
        nlom_p = [nlom[n * PAGE:(n + 1) * PAGE] for n in range(npages)]
        sums = jnp.concatenate([jnp.sum(x, axis=0, keepdims=True) for x in nlom_p], axis=0)
        return jnp.concatenate(nlom_p, axis=1).astype(BF16), lbeta, sums

    def stage_b(nl, lbeta, sums, get_v, npages):
        after = jnp.dot(u_ref[...], nl, preferred_element_type=F32)
        r = r_ref[...]
        a_t = []
        for n in range(npages):
            a = jnp.exp2(lbeta[n * PAGE:(n + 1) * PAGE] - after[:, n * LANES:(n + 1) * LANES] - r)
            a_t.append(a.T.astype(BF16))
            r = r + sums[n:n + 1]
        r_ref[...] = r
        a_t = jnp.concatenate(a_t, axis=1)
        for h in range(n_heads):
            v_h = jnp.concatenate([get_v(n, h) for n in range(npages)], axis=0)
            acc_ref[h] += jnp.dot(a_t[h * G:(h + 1) * G], v_h, preferred_element_type=F32)

    def pad_rows(x, before, total):
        parts = []
        if before:
            parts.append(jnp.zeros((before, x.shape[1]), x.dtype))
        parts.append(x)
        if total - before - x.shape[0]:
            parts.append(jnp.zeros((total - before - x.shape[0], x.shape[1]), x.dtype))
        return jnp.concatenate(parts, axis=0)

    @pl.when(pg == 0)
    def _():
        for h in range(n_heads):
            sl = slice(h * HEAD, (h + 1) * HEAD)
            qx_ref[h] = pad_rows(q_ref[:, sl] * (LOG2E * HEAD ** -0.5), h * G, LANES).astype(BF16)
        acc_ref[...] = jnp.zeros_like(acc_ref)
        r_ref[...] = jnp.zeros_like(r_ref)

        def new_rows(ref):
            return lambda n, h: pad_rows(ref[:, h * HEAD:(h + 1) * HEAD], 0, PAGE).astype(BF16)

        stage_b(*stage_a(new_rows(kn_ref), 1, True), new_rows(vn_ref), 1)
        nl_ref[1] = jnp.zeros(nl_ref.shape[1:], nl_ref.dtype)
        lb_ref[1] = jnp.full(lb_ref.shape[1:], MASKED_LOG2, lb_ref.dtype)
        cs_ref[1] = jnp.zeros(cs_ref.shape[1:], cs_ref.dtype)

    def pool_rows(refs):
        return lambda n, h: refs[n][pl.ds(h, PAGE, stride=n_heads), :].astype(BF16)

    done = lax.rem(pg + 1, 2)
    stage_b(nl_ref[done], lb_ref[done], cs_ref[done], pool_rows(vp_refs), ppb)
    cur = lax.rem(pg, 2)
    nl, lbeta, sums = stage_a(pool_rows(kp_refs), ppb, False)
    nl_ref[cur] = nl
    lb_ref[cur] = lbeta
    cs_ref[cur] = sums

    @pl.when(pg == pl.num_programs(1) - 1)
    def _():
        for h in range(n_heads):
            o_ref[:, h * HEAD:(h + 1) * HEAD] = acc_ref[h][:T_PAD].astype(o_ref.dtype)


def sb_sample(proj, bias, pool_k, pool_v, layer, page_table, t_valid, ppb, out_dtype):
    W = proj.shape[2]
    H = W // HEAD
    G = LANES // H
    B, n_pages = page_table.shape
    npg = n_pages // ppb
    bias_b = jnp.repeat(bias.astype(F32) * LOG2E, G).reshape(1, LANES)
    j = np.arange(PAGE)
    u_t = jnp.asarray((j[None, :] > j[:, None]).astype(np.float32), dtype=BF16)

    def rows_view(pool):
        return pool.reshape(pool.shape[0], pool.shape[1], PAGE * H, HEAD)

    def tok(g):
        return pl.BlockSpec((None, T_PAD, W), lambda b, p, pt: (g, b, 0))

    def pool_spec(n, lag):
        def index(b, p, pt):
            grp = jnp.clip(p - lag, 0, npg - 1)
            return layer, pt[b, n_pages - 1 - (grp * ppb + n)], 0, 0

        return pl.BlockSpec((None, None, PAGE * H, HEAD), index)

    grid_spec = pltpu.PrefetchScalarGridSpec(
        num_scalar_prefetch=1,
        grid=(B, npg + 1),
        in_specs=[tok(0), tok(1), tok(2),
                  pl.BlockSpec((1, LANES), lambda b, p, pt: (0, 0)),
                  pl.BlockSpec((PAGE, PAGE), lambda b, p, pt: (0, 0))]
        + [pool_spec(n, 0) for n in range(ppb)] + [pool_spec(n, 1) for n in range(ppb)],
        out_specs=pl.BlockSpec((T_PAD, W), lambda b, p, pt: (b, 0)),
        scratch_shapes=[pltpu.VMEM((H, LANES, HEAD), BF16), pltpu.VMEM((H, G, HEAD), F32),
                        pltpu.VMEM((1, LANES), F32),
                        pltpu.VMEM((2, PAGE, ppb * LANES), BF16), pltpu.VMEM((2, ppb * PAGE, LANES), F32),
                        pltpu.VMEM((2, ppb, LANES), F32)],
    )
    return pl.pallas_call(
        functools.partial(_sb_sample_body, n_heads=H, ppb=ppb, t_valid=t_valid),
        grid_spec=grid_spec,
        out_shape=jax.ShapeDtypeStruct((B * T_PAD, W), out_dtype),
        compiler_params=_cparams("parallel", "arbitrary"),
        name="sb_sample",
    )(page_table, proj, proj, proj, bias_b, u_t, *([rows_view(pool_k)] * ppb), *([rows_view(pool_v)] * ppb))


def _rope_tables(pos):
    half = HEAD // 2
    inv = 1.0 / (ROPE_BASE ** jnp.linspace(0.0, 1.0, half, dtype=F32))
    ang = pos[:, None] * inv[None, :]
    cos, sin = jnp.cos(ang), jnp.sin(ang)
    return jnp.concatenate([cos, cos], axis=-1), jnp.concatenate([-sin, sin], axis=-1)


def _retention_tables(n_heads, chunk):
    log_g = np.log(1.0 - 2.0 ** (-5.0 - np.arange(n_heads, dtype=np.float64)))
    idx = np.arange(HEAD, dtype=np.float64)
    diff = idx[:, None] - idx[None, :]
    dmask = np.where(diff >= 0, np.exp(log_g[:, None, None] * np.maximum(diff, 0.0)), 0.0)
    qdec = np.exp(log_g[:, None] * (idx + 1.0)[None, :])
    kdec = np.where(idx[None, :] < chunk, np.exp(log_g[:, None] * (chunk - 1.0 - idx)[None, :]), 0.0)
    bc = lambda t: jnp.asarray(np.broadcast_to(t[:, :, None], (n_heads, HEAD, HEAD)), dtype=F32)
    gch = tuple(float(x) for x in np.exp(log_g * chunk))
    return jnp.asarray(dmask, dtype=F32), bc(qdec), bc(kdec), gch


def kernel(x_prompt, x_sample, mem_prompt, state_ret, cache_sb_k, cache_sb_v, page_table, cache_mem_k, cache_mem_v, norm_mix_g, norm_x_g, norm_ffn_g, w_in_even, w_out_even, sg_norm_g, sg_w_s, sg_b, w_qkv_sb, w_o_sb, sb_bias, w_q_x, w_k_x, w_v_x, w_o_x, q_norm_x_g, k_norm_x_g, w_gu_dense, w_down_dense, w_router, w_gu_moe, w_down_moe):
    B, S, D = x_prompt.shape
    BS, T, _ = x_sample.shape
    depth = norm_mix_g.shape[0]
    n_heads_ret = state_ret.shape[2]
    n_experts = w_router.shape[2]
    past = page_table.shape[1] * PAGE
    H_sb = cache_sb_k.shape[3]
    W_sb = H_sb * HEAD
    W_x = w_q_x.shape[2]

    bf = lambda w: w.astype(BF16)
    TM = min(1024, B * S)
    MS = BS * T_PAD

    xp = x_prompt.reshape(B * S, D)
    xs = jnp.pad(x_sample, ((0, 0), (0, T_PAD - T), (0, 0))).reshape(MS, D)

    cos_p, sin_p = _rope_tables(jnp.arange(S, dtype=F32))
    cos_s, sin_s = _rope_tables(past + jnp.arange(HEAD, dtype=F32))
    dmask, qdec, kdec_p, gch_p = _retention_tables(n_heads_ret, HEAD)
    _, _, kdec_s, gch_s = _retention_tables(n_heads_ret, T)

    mk_p, mv_p, mk_p16, mv_p16 = mem_kv(mem_prompt, bf(w_k_x), bf(w_v_x), k_norm_x_g)
    n_f32 = max(depth - 2, 0)
    head_major = lambda c: jnp.transpose(c, (0, 1, 3, 2, 4))
    cmk, cmv = head_major(cache_mem_k[:n_f32]), head_major(cache_mem_v[:n_f32])
    cmk16, cmv16 = bf(head_major(cache_mem_k[n_f32:])), bf(head_major(cache_mem_v[n_f32:]))

    ret_p, ret_s, sgv_s = [], [], []
    kv_p = None
    sbk_s, sbv_s = [], []
    for l in range(depth):
        i = l // 2
        sample_f32 = l + 2 < depth
        if l % 2 == 0:
            w_in, w_out = bf(w_in_even[i]), bf(w_out_even[i])
            sgb = jnp.broadcast_to(sg_b[i][:, :, None], sg_b[i].shape + (HEAD,))
            proj = norm_mm(xp, norm_mix_g[l], w_in, 6, TM, 512)
            y, sp = even_core(proj, B, S, HEAD, cos_p, sin_p, dmask, qdec, kdec_p, gch_p,
                              sg_norm_g[i], sg_w_s[i], sgb, None, False, BF16)
            mix_p = (y, w_out)
            if sample_f32:
                w_in, w_out = w_in_even[i], w_out_even[i]
            proj = norm_mm(xs, norm_mix_g[l], w_in, 6, MS, 512)
            y, ss, vs = even_core(proj, BS, T_PAD, T_PAD, cos_s, sin_s, dmask, qdec, kdec_s, gch_s,
                                  sg_norm_g[i], sg_w_s[i], sgb, state_ret[i], True, w_in.dtype)
            xs = mm_res(y, w_out, xs, MS, 512)
            ret_p.append(sp)
            ret_s.append(ss)
            sgv_s.append(vs.reshape(BS, T_PAD, -1)[:, :T])
        else:
            w_qkv, w_o = bf(w_qkv_sb[i]), bf(w_o_sb[i])
            qkv16, *kv_p = norm_qkv(xp, norm_mix_g[l], w_qkv, LOG2E * HEAD ** -0.5, i, depth // 2, kv_p,
                                    TM)
            o = sb_prompt(qkv16, sb_bias[i], B, S, 256, 2)
            mix_p = (o, w_o)
            if sample_f32:
                w_qkv, w_o = w_qkv_sb[i], w_o_sb[i]
            proj = norm_mm(xs, norm_mix_g[l], w_qkv, 3, MS, 512)
            o = sb_sample(proj, sb_bias[i], cache_sb_k, cache_sb_v, i, page_table, T, math.gcd(16, page_table.shape[1]), w_o.dtype)
            xs = mm_res(o, w_o, xs, MS, 512)
            sbk_s.append(proj[1].reshape(BS, T_PAD, H_sb, HEAD)[:, :T])
            sbv_s.append(proj[2].reshape(BS, T_PAD, H_sb, HEAD)[:, :T])
        wq, wo = bf(w_q_x[l]), bf(w_o_x[l])
        xp = cross_attn(mix_p[0].reshape(B, S, -1), mix_p[1], xp.reshape(B, S, D), norm_x_g[l], wq, q_norm_x_g[l],
                        mk_p16, mv_p16, l, wo, 512).reshape(B * S, D)
        mem_s = (cmk16, cmv16, l - n_f32)
        if sample_f32:
            wq, wo, mem_s = w_q_x[l], w_o_x[l], (cmk, cmv, l)
        q_s = norm_mm(xs, norm_x_g[l], wq, 1, MS, 512)
        o_s = mem_attn(q_s.reshape(BS, T_PAD, -1), q_norm_x_g[l], *mem_s)
        xs = mm_res(o_s.reshape(MS, -1), wo, xs, MS, 512)
        if l % 2 == 0:
            w_gu, w_down = bf(w_gu_dense[i]), bf(w_down_dense[i])
            xp = ffn(xp, norm_ffn_g[l], w_gu, w_down, TM, 256)
            if sample_f32:
                w_gu, w_down = w_gu_dense[i], w_down_dense[i]
            xs = ffn(xs, norm_ffn_g[l], w_gu, w_down, MS, 256)
        else:
            w_gu, w_down = bf(w_gu_moe[i]), bf(w_down_moe[i])
            xp = moe_dense(xp, norm_ffn_g[l], w_router[i], w_gu, w_down, TM)
            xs = moe_dense(xs, norm_ffn_g[l], w_router[i], w_gu, w_down, MS)

    y_p = xp.reshape(B, S, D)
    y_s = xs.reshape(BS, T_PAD, D)[:, :T]
    return (y_p, y_s, jnp.stack(ret_p), jnp.stack(ret_s), jnp.stack(sgv_s),
            kv_p[0].reshape(-1, B, S, H_sb, HEAD), kv_p[1].reshape(-1, B, S, H_sb, HEAD),
            jnp.stack(sbk_s), jnp.stack(sbv_s), mk_p, mv_p)
```
